```python
import math
import jax, jax.numpy as jnp
from jax import lax
import numpy as np

D_MODEL = 2048
BATCH = 2
SEQ = 4096
DEPTH = 1
DEC_BATCH = 8
DEC_SEQ = 32
PAST_LEN = 2048

CHUNK = 64
EPS = 1e-6
D_CONV = D_MODEL // 2
CONV_WIDTH = 31
CONV_STATE = CONV_WIDTH - 1
N_HEADS = 8
D_HEAD = 64
D_VHEAD = 2 * D_HEAD
D_ATTN = N_HEADS * D_VHEAD
ROPE_THETA = 10000.0
Q_BLOCK = 128
NEG_INF = -1e30
PEER_HEADS = 8
PEER_NKEYS = 128
PEER_EXPERTS = PEER_NKEYS * PEER_NKEYS
PEER_DKEY = 128
PEER_TOPK = 16
PEER_BLOCK = 128
OFF_GLU_A = 0
OFF_GLU_B = OFF_GLU_A + D_CONV
OFF_Q = OFF_GLU_B + D_CONV
OFF_K = OFF_Q + N_HEADS * 2 * D_HEAD
OFF_V = OFF_K + N_HEADS * 2 * D_HEAD
OFF_GATE_C = OFF_V + D_ATTN
OFF_GATE_A = OFF_GATE_C + D_MODEL
D_IN_PROJ = OFF_GATE_A + D_MODEL

kernel_name = "hybrid_conformer_diffattn_peer_stream_step"


def lambda_init(layer):
    return 0.8 - 0.6 * math.exp(-0.3 * layer)


def rms_norm(x, g):
    xf = x.astype(jnp.float32)
    y = xf * lax.rsqrt(jnp.mean(xf * xf, axis=-1, keepdims=True) + EPS)
    return (y * g.astype(jnp.float32)).astype(x.dtype)


def layer_norm(x, g, b):
    xf = x.astype(jnp.float32)
    mu = jnp.mean(xf, axis=-1, keepdims=True)
    var = jnp.mean(jnp.square(xf - mu), axis=-1, keepdims=True)
    y = (xf - mu) * lax.rsqrt(var + EPS)
    return (y * g.astype(jnp.float32) + b.astype(jnp.float32)).astype(x.dtype)


def rope(x, pos):
    half = D_HEAD // 2
    inv = ROPE_THETA ** (-jnp.arange(half, dtype=jnp.float32) / half)
    ang = pos.astype(jnp.float32)[:, None] * inv[None, :]
    cos = jnp.cos(ang)[:, None, None, :]
    sin = jnp.sin(ang)[:, None, None, :]
    xf = x.astype(jnp.float32)
    x1, x2 = xf[..., :half], xf[..., half:]
    return jnp.concatenate([x1 * cos - x2 * sin, x2 * cos + x1 * sin], axis=-1).astype(x.dtype)


def diff_core(q, k, v, mask, lam):
    s = jnp.einsum('bqhcd,bkhcd->bhcqk', q, k).astype(jnp.float32) * (D_HEAD ** -0.5)
    if mask is not None:
        s = jnp.where(mask[None, None, None], s, NEG_INF)
    p = jax.nn.softmax(s, axis=-1)
    a = p[:, :, 0] - lam * p[:, :, 1]
    return jnp.einsum('bhqk,bkhe->bqhe', a.astype(v.dtype), v)


def attn_branch(z, pos, k_past, v_past, p, lam_init):
    B_, T_, _ = z.shape
    q = z[..., OFF_Q:OFF_K].reshape(B_, T_, N_HEADS, 2, D_HEAD)
    k = z[..., OFF_K:OFF_V].reshape(B_, T_, N_HEADS, 2, D_HEAD)
    v = z[..., OFF_V:OFF_GATE_C].reshape(B_, T_, N_HEADS, D_VHEAD)
    q = rope(rms_norm(q, p['q_norm_g']), pos)
    k = rope(rms_norm(k, p['k_norm_g']), pos)
    f32 = jnp.float32
    lam = (jnp.exp(jnp.sum(p['lambda_q1'].astype(f32) * p['lambda_k1'].astype(f32)))
           - jnp.exp(jnp.sum(p['lambda_q2'].astype(f32) * p['lambda_k2'].astype(f32)))
           + lam_init)
    if k_past is None:
        nb = T_ // Q_BLOCK
        qb = jnp.moveaxis(q.reshape(B_, nb, Q_BLOCK, N_HEADS, 2, D_HEAD), 1, 0)
        k_chunk = jnp.arange(T_) // CHUNK

        def blk(args):
            qi, i = args
            q_chunk = (i * Q_BLOCK + jnp.arange(Q_BLOCK)) // CHUNK
            mask = k_chunk[None, :] <= q_chunk[:, None]
            return diff_core(qi, k, v, mask, lam)

        o = lax.map(blk, (qb, jnp.arange(nb)))
        o = jnp.moveaxis(o, 0, 1).reshape(B_, T_, N_HEADS, D_VHEAD)
    else:
        k_all = jnp.concatenate([k_past, k], axis=1)
        v_all = jnp.concatenate([v_past, v], axis=1)
        o = diff_core(q, k_all, v_all, None, lam)
    o = rms_norm(o, p['subln_g']) * (1.0 - lam_init)
    return o.reshape(B_, T_, D_ATTN), k, v


def conv_branch(z, conv_hist, p):
    glu = z[..., OFF_GLU_A:OFF_GLU_B] * jax.nn.sigmoid(z[..., OFF_GLU_B:OFF_Q])
    u = jnp.concatenate([conv_hist.astype(glu.dtype), glu], axis=1)
    w = p['conv_dw_w'].reshape(CONV_WIDTH, 1, D_CONV).astype(u.dtype)
    c = lax.conv_general_dilated(u, w, window_strides=(1,), padding='VALID',
                                 dimension_numbers=('NWC', 'WIO', 'NWC'),
                                 feature_group_count=D_CONV)
    c = c + p['conv_dw_b']
    c = layer_norm(c, p['conv_ln_g'], p['conv_ln_b'])
    c = c * jax.nn.sigmoid(c)
    return c @ p['w_conv_out'], u[:, -CONV_STATE:]


def peer(h, w_pq, sub_k1, sub_k2, u_tab, v_tab):
    B_, T_, D = h.shape
    flat = h.reshape(-1, D)
    n = flat.shape[0]
    nblk = -(-n // PEER_BLOCK)
    flat = jnp.pad(flat, ((0, nblk * PEER_BLOCK - n), (0, 0)))
    f32 = jnp.float32
    half = PEER_DKEY // 2

    def blk(xb):
        q = (xb @ w_pq).reshape(PEER_BLOCK, PEER_HEADS, PEER_DKEY).astype(f32)
        s1 = jnp.einsum('thd,nd->thn', q[..., :half], sub_k1.astype(f32))
        s2 = jnp.einsum('thd,nd->thn', q[..., half:], sub_k2.astype(f32))
        v1, i1 = lax.top_k(s1, PEER_TOPK)
        v2, i2 = lax.top_k(s2, PEER_TOPK)
        cand = (v1[..., :, None] + v2[..., None, :]).reshape(PEER_BLOCK, PEER_HEADS, PEER_TOPK * PEER_TOPK)
        cidx = (i1[..., :, None] * PEER_NKEYS + i2[..., None, :]).reshape(PEER_BLOCK, PEER_HEADS, PEER_TOPK * PEER_TOPK)
        sc, j = lax.top_k(cand, PEER_TOPK)
        idx = jnp.take_along_axis(cidx, j, axis=-1)
        g = jax.nn.softmax(sc, axis=-1)
        ue = u_tab[idx]
        ve = v_tab[idx]
        a = jnp.einsum('td,thkd->thk', xb, ue).astype(f32)
        wgt = (g * jax.nn.gelu(a, approximate=False)).astype(xb.dtype)
        return jnp.einsum('thk,thkd->td', wgt, ve)

    out = lax.map(blk, flat.reshape(nblk, PEER_BLOCK, D))
    return out.reshape(-1, D)[:n].reshape(B_, T_, D)


def layer(x, pos, conv_hist, k_past, v_past, p, lam_init):
    h = rms_norm(x, p['norm1_g'])
    z = h @ p['w_in']
    conv_out, conv_state = conv_branch(z, conv_hist, p)
    attn_o, k_new, v_new = attn_branch(z, pos, k_past, v_past, p, lam_init)
    attn_out = attn_o @ p['w_attn_out']
    merged = (jax.nn.sigmoid(z[..., OFF_GATE_C:OFF_GATE_A]) * conv_out
              + jax.nn.sigmoid(z[..., OFF_GATE_A:D_IN_PROJ]) * attn_out)
    x = x + merged @ p['w_out']
    h2 = rms_norm(x, p['norm2_g'])
    x = x + peer(h2, p['peer_wq'], p['peer_k1'], p['peer_k2'], p['peer_u'], p['peer_v'])
    return x, k_new, v_new, conv_state


def setup_inputs(seed: int = 0) -> dict:
    key = jax.random.key(seed)
    ks = jax.random.split(key, 32)
    nrm = jax.random.normal
    f = jnp.float32
    L = DEPTH
    return {
        'x_prompt': nrm(ks[0], (BATCH, SEQ, D_MODEL), f),
        'x_sample': nrm(ks[1], (DEC_BATCH, DEC_SEQ, D_MODEL), f),
        'cache_attn_k': nrm(ks[2], (L, DEC_BATCH, PAST_LEN, N_HEADS, 2, D_HEAD), f),
        'cache_attn_v': nrm(ks[3], (L, DEC_BATCH, PAST_LEN, N_HEADS, D_VHEAD), f),
        'state_conv': nrm(ks[4], (L, DEC_BATCH, CONV_STATE, D_CONV), f) * 0.5,
        'norm1_g': 1.0 + 0.02 * nrm(ks[5], (L, D_MODEL), f),
        'w_in': nrm(ks[6], (L, D_MODEL, D_IN_PROJ), f) * D_MODEL ** -0.5,
        'conv_dw_w': nrm(ks[7], (L, CONV_WIDTH, D_CONV), f) * CONV_WIDTH ** -0.5,
        'conv_dw_b': 0.02 * nrm(ks[8], (L, D_CONV), f),
        'conv_ln_g': 1.0 + 0.02 * nrm(ks[9], (L, D_CONV), f),
        'conv_ln_b': 0.02 * nrm(ks[10], (L, D_CONV), f),
        'w_conv_out': nrm(ks[11], (L, D_CONV, D_MODEL), f) * D_CONV ** -0.5,
        'q_norm_g': 1.0 + 0.02 * nrm(ks[12], (L, D_HEAD), f),
        'k_norm_g': 1.0 + 0.02 * nrm(ks[13], (L, D_HEAD), f),
        'lambda_q1': 0.1 * nrm(ks[14], (L, D_HEAD), f),
        'lambda_k1': 0.1 * nrm(ks[15], (L, D_HEAD), f),
        'lambda_q2': 0.1 * nrm(ks[16], (L, D_HEAD), f),
        'lambda_k2': 0.1 * nrm(ks[17], (L, D_HEAD), f),
        'subln_g': 1.0 + 0.02 * nrm(ks[18], (L, D_VHEAD), f),
        'w_attn_out': nrm(ks[19], (L, D_ATTN, D_MODEL), f) * D_ATTN ** -0.5,
        'w_out': nrm(ks[20], (L, D_MODEL, D_MODEL), f) * D_MODEL ** -0.5,
        'norm2_g': 1.0 + 0.02 * nrm(ks[21], (L, D_MODEL), f),
        'peer_wq': nrm(ks[22], (L, D_MODEL, PEER_HEADS * PEER_DKEY), f) * D_MODEL ** -0.5,
        'peer_k1': nrm(ks[23], (L, PEER_NKEYS, PEER_DKEY // 2), f) * (PEER_DKEY // 2) ** -0.5,
        'peer_k2': nrm(ks[24], (L, PEER_NKEYS, PEER_DKEY // 2), f) * (PEER_DKEY // 2) ** -0.5,
        'peer_u': nrm(ks[25], (L, PEER_EXPERTS, D_MODEL), f) * D_MODEL ** -0.5,
        'peer_v': nrm(ks[26], (L, PEER_EXPERTS, D_MODEL), f) * 0.5,
    }


def reference(x_prompt, x_sample, cache_attn_k, cache_attn_v, state_conv,
              norm1_g, w_in, conv_dw_w, conv_dw_b, conv_ln_g, conv_ln_b, w_conv_out,
              q_norm_g, k_norm_g, lambda_q1, lambda_k1, lambda_q2, lambda_k2, subln_g,
              w_attn_out, w_out, norm2_g, peer_wq, peer_k1, peer_k2, peer_u, peer_v):
    pos_p = jnp.arange(x_prompt.shape[1])
    pos_s = x_sample.shape[1] * 0 + PAST_LEN + jnp.arange(x_sample.shape[1])
    xp, xs = x_prompt, x_sample
    kp_l, vp_l, cp_l, ksl, vsl, csl = [], [], [], [], [], []
    for l in range(DEPTH):
        p = dict(norm1_g=norm1_g[l], w_in=w_in[l], conv_dw_w=conv_dw_w[l], conv_dw_b=conv_dw_b[l],
                 conv_ln_g=conv_ln_g[l], conv_ln_b=conv_ln_b[l], w_conv_out=w_conv_out[l],
                 q_norm_g=q_norm_g[l], k_norm_g=k_norm_g[l], lambda_q1=lambda_q1[l],
                 lambda_k1=lambda_k1[l], lambda_q2=lambda_q2[l], lambda_k2=lambda_k2[l],
                 subln_g=subln_g[l], w_attn_out=w_attn_out[l], w_out=w_out[l], norm2_g=norm2_g[l],
                 peer_wq=peer_wq[l], peer_k1=peer_k1[l], peer_k2=peer_k2[l],
                 peer_u=peer_u[l], peer_v=peer_v[l])
        li = lambda_init(l)
        zero_hist = jnp.zeros((xp.shape[0], CONV_STATE, D_CONV), xp.dtype)
        xp, kp, vp, cp = layer(xp, pos_p, zero_hist, None, None, p, li)
        xs, ks_, vs_, cs_ = layer(xs, pos_s, state_conv[l], cache_attn_k[l], cache_attn_v[l], p, li)
        kp_l.append(kp); vp_l.append(vp); cp_l.append(cp)
        ksl.append(ks_); vsl.append(vs_); csl.append(cs_)
    k_prompt = jnp.stack(kp_l); v_prompt = jnp.stack(vp_l); conv_prompt = jnp.stack(cp_l)
    k_sample = jnp.stack(ksl); v_sample = jnp.stack(vsl); conv_sample = jnp.stack(csl)
    return (xp, xs, k_prompt, v_prompt, conv_prompt, k_sample, v_sample, conv_sample)
```

```python
import functools
import math

import jax
import jax.numpy as jnp
from jax import lax
from jax.experimental import pallas as pl
from jax.experimental.pallas import tpu as pltpu

F32 = jnp.float32
BF16 = jnp.bfloat16
I32 = jnp.int32

LANES = 128
CHUNK = 64
EPS = 1e-6
N_HEADS = 8
D_HEAD = 64
D_VHEAD = 2 * D_HEAD
CONV_WIDTH = 31
CONV_STATE = CONV_WIDTH - 1
CONV_PAD = 32
ROPE_THETA = 10000.0
NEG_INF = -1e30
PEER_HEADS = 8
PEER_TOPK = 16
VMEM_LIMIT = 56 * 1024 * 1024

NT_DIMS = (((1,), (1,)), ((), ()))
TN_DIMS = (((0,), (0,)), ((), ()))


def _cparams(sem):
    return pltpu.CompilerParams(dimension_semantics=sem, vmem_limit_bytes=VMEM_LIMIT)


def _lambda_init(layer):
    return 0.8 - 0.6 * math.exp(-0.3 * layer)


def _qk_norm_rope(z, g, cos, sin_signed):
    tm = z.shape[0]
    lane = lax.broadcasted_iota(I32, (tm, LANES), 1)
    lo = lane < D_HEAD
    first = (lane & (D_HEAD // 2)) == 0
    outs = []
    for c in range(z.shape[1] // LANES):
        zc = z[:, c * LANES:(c + 1) * LANES]
        zz = zc * zc
        s_lo = jnp.sum(jnp.where(lo, zz, 0.0), axis=-1, keepdims=True)
        s_hi = jnp.sum(jnp.where(lo, 0.0, zz), axis=-1, keepdims=True)
        r = jnp.where(lo, lax.rsqrt(s_lo * (1.0 / D_HEAD) + EPS), lax.rsqrt(s_hi * (1.0 / D_HEAD) + EPS))
        y = zc * r * g
        up = pltpu.roll(y, LANES - D_HEAD // 2, 1)
        dn = pltpu.roll(y, D_HEAD // 2, 1)
        outs.append(y * cos + jnp.where(first, up, dn) * sin_signed)
    return jnp.concatenate(outs, axis=1)


def _in_proj_kernel(x_ref, g1_ref, w_ref, qg_ref, kg_ref,
                    glu_ref, q_ref, k_ref, kb_ref, v_ref, vb_ref, gate_ref,
                    h_scr, za_scr, cos_scr, sin_scr, *, seq_len, pos_offset, tm):
    i = pl.program_id(0)
    j = pl.program_id(1)

    @pl.when(j == 0)
    def _():
        x = x_ref[...]
        ms = jnp.mean(x * x, axis=-1, keepdims=True)
        h_scr[...] = (x * lax.rsqrt(ms + EPS) * g1_ref[...]).astype(BF16)
        row = lax.broadcasted_iota(I32, (tm, LANES), 0) + i * tm
        pos = (row & (seq_len - 1)) + pos_offset
        lane = lax.broadcasted_iota(I32, (tm, LANES), 1)
        f = (lane & (D_HEAD // 2 - 1)).astype(F32)
        inv = jnp.power(jnp.float32(ROPE_THETA), -f / (D_HEAD // 2))
        ang = pos.astype(F32) * inv
        cos_scr[...] = jnp.cos(ang)
        s = jnp.sin(ang)
        sin_scr[...] = jnp.where((lane & (D_HEAD // 2)) == 0, -s, s)

    z = jnp.dot(h_scr[...], w_ref[...], preferred_element_type=F32)

    @pl.when(j == 0)
    def _():
        za_scr[...] = z

    @pl.when(j == 1)
    def _():
        glu_ref[...] = za_scr[...] * jax.nn.sigmoid(z)

    @pl.when(j == 2)
    def _():
        q = _qk_norm_rope(z, qg_ref[...], cos_scr[...], sin_scr[...])
        q_ref[...] = (q * (D_HEAD ** -0.5)).astype(BF16)

    @pl.when(j == 3)
    def _():
        k = _qk_norm_rope(z, kg_ref[...], cos_scr[...], sin_scr[...])
        k_ref[...] = k
        kb_ref[...] = k.astype(BF16)

    @pl.when(j == 4)
    def _():
        v_ref[...] = z
        vb_ref[...] = z.astype(BF16)

    @pl.when(j >= 5)
    def _():
        gate_ref[...] = jax.nn.sigmoid(z).astype(BF16)


def _in_proj(x2d, g1, w_in_bf, qg, kg, *, seq_len, pos_offset, tm):
    n, d = x2d.shape
    d_in = w_in_bf.shape[1]
    tn = d // 2
    nj = d_in // tn
    assert n % tm == 0 and d_in % tn == 0 and nj == 9
    assert seq_len & (seq_len - 1) == 0
    blk = lambda: pl.BlockSpec((tm, tn), lambda i, j: (i, 0))
    out_shape = (
        jax.ShapeDtypeStruct((n, tn), F32),
        jax.ShapeDtypeStruct((n, tn), BF16),
        jax.ShapeDtypeStruct((n, tn), F32),
        jax.ShapeDtypeStruct((n, tn), BF16),
        jax.ShapeDtypeStruct((n, tn), F32),
        jax.ShapeDtypeStruct((n, tn), BF16),
        jax.ShapeDtypeStruct((n, 4 * tn), BF16),
    )
    return pl.pallas_call(
        functools.partial(_in_proj_kernel, seq_len=seq_len, pos_offset=pos_offset, tm=tm),
        grid=(n // tm, nj),
        in_specs=[
            pl.BlockSpec((tm, d), lambda i, j: (i, 0)),
            pl.BlockSpec((1, d), lambda i, j: (0, 0)),
            pl.BlockSpec((d, tn), lambda i, j: (0, j)),
            pl.BlockSpec((1, LANES), lambda i, j: (0, 0)),
            pl.BlockSpec((1, LANES), lambda i, j: (0, 0)),
        ],
        out_specs=(blk(), blk(), blk(), blk(), blk(), blk(),
                   pl.BlockSpec((tm, tn), lambda i, j: (i, jnp.maximum(j - 5, 0)))),
        out_shape=out_shape,
        scratch_shapes=[
            pltpu.VMEM((tm, d), BF16),
            pltpu.VMEM((tm, tn), F32),
            pltpu.VMEM((tm, LANES), F32),
            pltpu.VMEM((tm, LANES), F32),
        ],
        compiler_params=_cparams(("parallel", "arbitrary")),
        name="in_proj",
    )(x2d, g1, w_in_bf, qg, kg)


CONV_ROWS = 64


def _conv_kernel(cur_ref, prev_ref, st_ref, w_ref, b_ref, lg_ref, lb_ref,
                 c_ref, so_ref, buf, *, tt):
    ti = pl.program_id(1)
    dc = cur_ref.shape[2]

    @pl.when(ti == 0)
    def _():
        buf[0:CONV_PAD, :] = st_ref[0]

    @pl.when(ti > 0)
    def _():
        buf[0:CONV_PAD, :] = prev_ref[0]

    buf[CONV_PAD:CONV_PAD + tt, :] = cur_ref[0]
    lead = CONV_PAD - CONV_STATE
    rows = min(CONV_ROWS, tt)

    for rs in range(tt // rows):
        parts = []
        for cs in range(dc // LANES):
            sl = slice(cs * LANES, (cs + 1) * LANES)
            acc = jnp.zeros((rows, LANES), F32)
            for j in range(CONV_WIDTH):
                r0 = lead + rs * rows + j
                acc = acc + buf[r0:r0 + rows, sl] * w_ref[j:j + 1, sl]
            parts.append(acc + b_ref[:, sl])
        c = jnp.concatenate(parts, axis=1)
        mu = jnp.mean(c, axis=-1, keepdims=True)
        var = jnp.mean(jnp.square(c - mu), axis=-1, keepdims=True)
        y = (c - mu) * lax.rsqrt(var + EPS) * lg_ref[...] + lb_ref[...]
        c_ref[0, rs * rows:(rs + 1) * rows, :] = (y * jax.nn.sigmoid(y)).astype(BF16)

    @pl.when(ti == pl.num_programs(1) - 1)
    def _():
        so_ref[0] = buf[lead + tt:lead + tt + CONV_STATE, :]


def _conv(glu3d, hist, w, b, lg, lb, *, tt):
    bsz, t, dc = glu3d.shape
    assert t % tt == 0 and tt % min(CONV_ROWS, tt) == 0 and tt % CONV_PAD == 0
    per = tt // CONV_PAD
    return pl.pallas_call(
        functools.partial(_conv_kernel, tt=tt),
        grid=(bsz, t // tt),
        in_specs=[
            pl.BlockSpec((1, tt, dc), lambda bi, ti: (bi, ti, 0)),
            pl.BlockSpec((1, CONV_PAD, dc), lambda bi, ti: (bi, jnp.maximum(ti * per - 1, 0), 0)),
            pl.BlockSpec((1, CONV_PAD, dc), lambda bi, ti: (bi, 0, 0)),
            pl.BlockSpec((CONV_WIDTH, dc), lambda bi, ti: (0, 0)),
            pl.BlockSpec((1, dc), lambda bi, ti: (0, 0)),
            pl.BlockSpec((1, dc), lambda bi, ti: (0, 0)),
            pl.BlockSpec((1, dc), lambda bi, ti: (0, 0)),
        ],
        out_specs=(
            pl.BlockSpec((1, tt, dc), lambda bi, ti: (bi, ti, 0)),
            pl.BlockSpec((1, CONV_STATE, dc), lambda bi, ti: (bi, 0, 0)),
        ),
        out_shape=(
            jax.ShapeDtypeStruct((bsz, t, dc), BF16),
            jax.ShapeDtypeStruct((bsz, CONV_STATE, dc), F32),
        ),
        scratch_shapes=[pltpu.VMEM((CONV_PAD + tt, dc), F32)],
        compiler_params=_cparams(("parallel", "arbitrary")),
        name="conv_branch",
    )(glu3d, glu3d, hist, w, b, lg, lb)


def _lambda_value(lq1_ref, lk1_ref, lq2_ref, lk2_ref, lam_init):
    a = jnp.sum(lq1_ref[...] * lk1_ref[...], axis=-1, keepdims=True)
    b = jnp.sum(lq2_ref[...] * lk2_ref[...], axis=-1, keepdims=True)
    return jnp.exp(a) - jnp.exp(b) + lam_init


def _head_finish(o1, o2, lam, sg, lam_init):
    o = o1 - lam * o2
    ms = jnp.mean(o * o, axis=-1, keepdims=True)
    return (o * lax.rsqrt(ms + EPS) * sg) * (1.0 - lam_init)


def _split_heads(q):
    lane = lax.broadcasted_iota(I32, q.shape, 1)
    zero = jnp.zeros_like(q)
    return jnp.where(lane < D_HEAD, q, zero), jnp.where(lane >= D_HEAD, q, zero)


def _attn_prompt_kernel(qi_tab, ki_tab, q_ref, k_ref, v_ref, lq1_ref, lk1_ref, lq2_ref, lk2_ref, sg_ref,
                        o_ref, m_scr, l_scr, acc_scr, *, tq, tk, lam_init):
    s_id = pl.program_id(2)
    qi = qi_tab[s_id]
    ki = ki_tab[s_id]

    @pl.when(ki == 0)
    def _():
        m_scr[...] = jnp.full(m_scr.shape, -jnp.inf, F32)
        l_scr[...] = jnp.zeros(l_scr.shape, F32)
        acc_scr[...] = jnp.zeros(acc_scr.shape, F32)

    q = q_ref[0]
    k = k_ref[0]
    v = v_ref[0]
    row_chunk = (lax.broadcasted_iota(I32, (tq, tk), 0) + qi * tq) // CHUNK
    col_chunk = (lax.broadcasted_iota(I32, (tq, tk), 1) + ki * tk) // CHUNK
    mask = col_chunk <= row_chunk
    for c, qc in enumerate(_split_heads(q)):
        s = lax.dot_general(qc, k, NT_DIMS, preferred_element_type=F32)
        s = jnp.where(mask, s, NEG_INF)
        m_prev = m_scr[c]
        m_new = jnp.maximum(m_prev, jnp.max(s, axis=-1, keepdims=True))
        alpha = jnp.exp(m_prev - m_new)
        p = jnp.exp(s - jnp.concatenate([m_new] * (tk // LANES), axis=1))
        l_scr[c] = alpha * l_scr[c] + jnp.sum(p, axis=-1, keepdims=True)
        acc_scr[c] = alpha * acc_scr[c] + jnp.dot(p.astype(BF16), v, preferred_element_type=F32)
        m_scr[c] = m_new

    @pl.when(ki == qi)
    def _():
        lam = _lambda_value(lq1_ref, lk1_ref, lq2_ref, lk2_ref, lam_init)
        y = _head_finish(acc_scr[0] / l_scr[0], acc_scr[1] / l_scr[1], lam, sg_ref[...], lam_init)
        o_ref[0] = y.astype(BF16)


def _attn_prompt(q3, k3, v3, lq1, lk1, lq2, lk2, sg, *, tq, lam_init):
    bsz, t, da = q3.shape
    nq = t // tq
    assert t % tq == 0 and tq % CHUNK == 0
    pairs = [(a, b) for a in range(nq) for b in range(a + 1)]
    qi_tab = jnp.asarray([p[0] for p in pairs], I32)
    ki_tab = jnp.asarray([p[1] for p in pairs], I32)
    vec = lambda n: pl.BlockSpec((1, n), lambda b, h, s, qt, kt: (0, 0))
    grid_spec = pltpu.PrefetchScalarGridSpec(
        num_scalar_prefetch=2,
        grid=(bsz, N_HEADS, len(pairs)),
        in_specs=[
            pl.BlockSpec((1, tq, LANES), lambda b, h, s, qt, kt: (b, qt[s], h)),
            pl.BlockSpec((1, tq, LANES), lambda b, h, s, qt, kt: (b, kt[s], h)),
            pl.BlockSpec((1, tq, LANES), lambda b, h, s, qt, kt: (b, kt[s], h)),
            vec(D_HEAD), vec(D_HEAD), vec(D_HEAD), vec(D_HEAD), vec(D_VHEAD),
        ],
        out_specs=pl.BlockSpec((1, tq, LANES), lambda b, h, s, qt, kt: (b, qt[s], h)),
        scratch_shapes=[
            pltpu.VMEM((2, tq, LANES), F32),
            pltpu.VMEM((2, tq, LANES), F32),
            pltpu.VMEM((2, tq, D_VHEAD), F32),
        ],
    )
    return pl.pallas_call(
        functools.partial(_attn_prompt_kernel, tq=tq, tk=tq, lam_init=lam_init),
        grid_spec=grid_spec,
        out_shape=jax.ShapeDtypeStruct((bsz, t, da), BF16),
        compiler_params=_cparams(("parallel", "parallel", "arbitrary")),
        name="attn_prompt",
    )(qi_tab, ki_tab, q3, k3, v3, lq1, lk1, lq2, lk2, sg)


def _attn_sample_kernel(q_ref, kn_ref, vn_ref, ck_ref, cv_ref, lq1_ref, lk1_ref, lq2_ref, lk2_ref, sg_ref,
                        o_ref, *, lam_init):
    q = q_ref[0]
    kn = kn_ref[0]
    vn = vn_ref[0]
    kp = ck_ref[0].astype(BF16)
    vp = cv_ref[0].astype(BF16)
    outs = []
    for qc in _split_heads(q):
        sp = lax.dot_general(qc, kp, NT_DIMS, preferred_element_type=F32)
        sn = lax.dot_general(qc, kn, NT_DIMS, preferred_element_type=F32)
        m = jnp.maximum(jnp.max(sp, axis=-1, keepdims=True), jnp.max(sn, axis=-1, keepdims=True))
        pp = jnp.exp(sp - m)
        pn = jnp.exp(sn - m)
        l = jnp.sum(pp, axis=-1, keepdims=True) + jnp.sum(pn, axis=-1, keepdims=True)
        o = (jnp.dot(pp.astype(BF16), vp, preferred_element_type=F32)
             + jnp.dot(pn.astype(BF16), vn, preferred_element_type=F32))
        outs.append(o / l)
    lam = _lambda_value(lq1_ref, lk1_ref, lq2_ref, lk2_ref, lam_init)
    o_ref[0] = _head_finish(outs[0], outs[1], lam, sg_ref[...], lam_init).astype(BF16)


def _attn_sample(q3, kn3, vn3, ck3, cv3, lq1, lk1, lq2, lk2, sg, *, lam_init):
    bsz, t, da = q3.shape
    past = ck3.shape[1]
    vec = lambda n: pl.BlockSpec((1, n), lambda b, h: (0, 0))
    new = lambda: pl.BlockSpec((1, t, LANES), lambda b, h: (b, 0, h))
    old = lambda: pl.BlockSpec((1, past, LANES), lambda b, h: (b, 0, h))
    return pl.pallas_call(
        functools.partial(_attn_sample_kernel, lam_init=lam_init),
        grid=(bsz, N_HEADS),
        in_specs=[new(), new(), new(), old(), old(),
                  vec(D_HEAD), vec(D_HEAD), vec(D_HEAD), vec(D_HEAD), vec(D_VHEAD)],
        out_specs=new(),
        out_shape=jax.ShapeDtypeStruct((bsz, t, da), BF16),
        compiler_params=_cparams(("parallel", "parallel")),
        name="attn_sample",
    )(q3, kn3, vn3, ck3, cv3, lq1, lk1, lq2, lk2, sg)


def _merge_kernel(c_ref, o_ref, gc_ref, ga_ref, wco_ref, wao_ref, wout_ref, x_ref, y_ref):
    j = pl.program_id(1)

    @pl.when(j == 0)
    def _():
        y_ref[...] = x_ref[...]

    co = jnp.dot(c_ref[...], wco_ref[...], preferred_element_type=F32)
    ao = jnp.dot(o_ref[...], wao_ref[...], preferred_element_type=F32)
    merged = gc_ref[...].astype(F32) * co + ga_ref[...].astype(F32) * ao
    y_ref[...] += jnp.dot(merged.astype(BF16), wout_ref[...], preferred_element_type=F32)


def _merge(c2d, o2d, gates, wco, wao, wout, x2d, *, tm):
    n, d = x2d.shape
    dh = d // 2
    assert n % tm == 0
    return pl.pallas_call(
        _merge_kernel,
        grid=(n // tm, 2),
        in_specs=[
            pl.BlockSpec((tm, dh), lambda i, j: (i, 0)),
            pl.BlockSpec((tm, dh), lambda i, j: (i, 0)),
            pl.BlockSpec((tm, dh), lambda i, j: (i, j)),
            pl.BlockSpec((tm, dh), lambda i, j: (i, 2 + j)),
            pl.BlockSpec((dh, dh), lambda i, j: (0, j)),
            pl.BlockSpec((dh, dh), lambda i, j: (0, j)),
            pl.BlockSpec((dh, d), lambda i, j: (j, 0)),
            pl.BlockSpec((tm, d), lambda i, j: (i, 0)),
        ],
        out_specs=pl.BlockSpec((tm, d), lambda i, j: (i, 0)),
        out_shape=jax.ShapeDtypeStruct((n, d), F32),
        compiler_params=_cparams(("parallel", "arbitrary")),
        name="merge_out_proj",
    )(c2d, o2d, gates, gates, wco, wao, wout, x2d)


def _topk_rows(cur_ref, rank_ref, val_ref, idx_ref, k):
    rows = cur_ref.shape[0]
    iota = lax.broadcasted_iota(I32, cur_ref.shape, 0).astype(F32)
    if rank_ref is not None:
        rank_ref[...] = jnp.full(rank_ref.shape, float(k), F32)

    def body(j, carry):
        cur = cur_ref[...]
        m = jnp.max(cur, axis=0, keepdims=True)
        idx = jnp.min(jnp.where(cur == m, iota, float(rows)), axis=0, keepdims=True)
        hit = iota == idx
        if rank_ref is not None:
            rank_ref[...] = jnp.where(hit, j.astype(F32), rank_ref[...])
        cur_ref[...] = jnp.where(hit, -jnp.inf, cur)
        val_ref[j] = m
        idx_ref[j] = idx
        return carry

    lax.fori_loop(0, k, body, 0)


def _peer_kernel(x_ref, g2_ref, wq_ref, k1_ref, k2_ref, u_ref, v_ref, y_ref,
                 h_scr, q_scr, e2_scr, rank2_scr, c_scr, n_scr,
                 cur_scr, rank_scr, cand_scr, val1_scr, val2_scr, valc_scr, idx_scr,
                 *, tm, ce):
    c_id = pl.program_id(1)
    nk = k1_ref.shape[0]
    kk = PEER_TOPK

    @pl.when(c_id == 0)
    def _():
        x = x_ref[...]
        y_ref[...] = x
        ms = jnp.mean(x * x, axis=-1, keepdims=True)
        h = (x * lax.rsqrt(ms + EPS) * g2_ref[...]).astype(BF16)
        h_scr[...] = h
        q = jnp.dot(h, wq_ref[...], preferred_element_type=F32)
        for hd in range(PEER_HEADS):
            q_scr[hd] = q[:, hd * LANES:(hd + 1) * LANES]

        def head_body(hd, carry):
            qh = q_scr[hd]
            s1 = lax.dot_general(k1_ref[...], qh, NT_DIMS, preferred_element_type=F32)
            s2 = lax.dot_general(k2_ref[...], qh, NT_DIMS, preferred_element_type=F32)
            cur_scr[...] = s1
            _topk_rows(cur_scr, rank_scr, val1_scr, idx_scr, kk)
            rank1 = rank_scr[...]
            cur_scr[...] = s2
            _topk_rows(cur_scr, rank2_scr.at[hd], val2_scr, idx_scr, kk)
            v1max = val1_scr[0]
            v2max = val2_scr[0]
            v2all = jnp.concatenate([val2_scr[j2] for j2 in range(kk)], axis=0)
            for j1 in range(kk):
                cand_scr[j1 * kk:(j1 + 1) * kk, :] = val1_scr[j1] + v2all
            _topk_rows(cand_scr, None, valc_scr, idx_scr, kk)
            top = valc_scr[0]
            z = jnp.zeros((1, tm), F32)
            row16 = lax.broadcasted_iota(I32, (kk, tm), 0).astype(F32)
            n16 = jnp.zeros((kk, tm), F32)
            for j in range(kk):
                z = z + jnp.exp(valc_scr[j] - top)
                n16 = n16 + jnp.where(row16 == jnp.floor(idx_scr[j] * (1.0 / kk)), 1.0, 0.0)
            n_by = jnp.zeros((nk, tm), F32)
            for j1 in range(kk):
                n_by = n_by + jnp.where(rank1 == float(j1), n16[j1:j1 + 1, :], 0.0)
            n_scr[hd] = n_by
            c_scr[hd] = jnp.exp(s1 - v1max) / z
            e2_scr[hd] = jnp.exp(s2 - v2max)
            return carry

        lax.fori_loop(0, PEER_HEADS, head_body, 0)

    at = lax.dot_general(u_ref[...], h_scr[...], NT_DIMS, preferred_element_type=F32)
    parts = []
    for rr in range(ce // nk):
        r = c_id * (ce // nk) + rr
        acc = jnp.zeros((nk, tm), F32)
        for hd in range(PEER_HEADS):
            n_row = n_scr[hd, pl.ds(r, 1), :]
            c_row = c_scr[hd, pl.ds(r, 1), :]
            acc = acc + jnp.where(rank2_scr[hd] < n_row, e2_scr[hd] * c_row, 0.0)
        a = at[rr * nk:(rr + 1) * nk, :]
        gelu = 0.5 * a * (1.0 + lax.erf(a * (2.0 ** -0.5)))
        parts.append((acc * gelu).astype(BF16))
    wt = jnp.concatenate(parts, axis=0)
    y_ref[...] += lax.dot_general(wt, v_ref[...], TN_DIMS, preferred_element_type=F32)


def _peer(x2d, g2, wq, k1p, k2p, u_bf, v_bf, *, tm, ce):
    n, d = x2d.shape
    ne = u_bf.shape[0]
    nk = k1p.shape[0]
    kk = PEER_TOPK
    assert n % tm == 0 and ne % ce == 0 and ce % nk == 0 and ne == nk * nk and nk == LANES
    return pl.pallas_call(
        functools.partial(_peer_kernel, tm=tm, ce=ce),
        grid=(n // tm, ne // ce),
        in_specs=[
            pl.BlockSpec((tm, d), lambda i, c: (i, 0)),
            pl.BlockSpec((1, d), lambda i, c: (0, 0)),
            pl.BlockSpec(wq.shape, lambda i, c: (0, 0)),
            pl.BlockSpec((nk, LANES), lambda i, c: (0, 0)),
            pl.BlockSpec((nk, LANES), lambda i, c: (0, 0)),
            pl.BlockSpec((ce, d), lambda i, c: (c, 0)),
            pl.BlockSpec((ce, d), lambda i, c: (c, 0)),
        ],
        out_specs=pl.BlockSpec((tm, d), lambda i, c: (i, 0)),
        out_shape=jax.ShapeDtypeStruct((n, d), F32),
        scratch_shapes=[
            pltpu.VMEM((tm, d), BF16),
            pltpu.VMEM((PEER_HEADS, tm, LANES), F32),
            pltpu.VMEM((PEER_HEADS, nk, tm), F32),
            pltpu.VMEM((PEER_HEADS, nk, tm), F32),
            pltpu.VMEM((PEER_HEADS, nk, tm), F32),
            pltpu.VMEM((PEER_HEADS, nk, tm), F32),
            pltpu.VMEM((nk, tm), F32),
            pltpu.VMEM((nk, tm), F32),
            pltpu.VMEM((kk * kk, tm), F32),
            pltpu.VMEM((kk, 1, tm), F32),
            pltpu.VMEM((kk, 1, tm), F32),
            pltpu.VMEM((kk, 1, tm), F32),
            pltpu.VMEM((kk, 1, tm), F32),
        ],
        compiler_params=_cparams(("parallel", "arbitrary")),
        name="peer",
    )(x2d, g2, wq, k1p, k2p, u_bf, v_bf)


def _tile128(g):
    return jnp.tile(g.reshape(1, -1), (1, LANES // g.shape[-1]))


def _layer(x3, pos_offset, hist, cache_k, cache_v, p, lam_init, *, tm, tt, tq, tm_peer, ce):
    bsz, t, d = x3.shape
    n = bsz * t
    x2d = x3.reshape(n, d)
    glu, q, k, kb, v, vb, gates = _in_proj(x2d, p["norm1_g"], p["w_in"], p["qg"], p["kg"],
                                           seq_len=t, pos_offset=pos_offset, tm=tm)
    dc = glu.shape[1]
    c, conv_state = _conv(glu.reshape(bsz, t, dc), hist, p["conv_dw_w"], p["conv_dw_b"],
                          p["conv_ln_g"], p["conv_ln_b"], tt=tt)
    lam_args = (p["lambda_q1"], p["lambda_k1"], p["lambda_q2"], p["lambda_k2"], p["subln_g"])
    r3 = lambda a: a.reshape(bsz, t, -1)
    if cache_k is None:
        o = _attn_prompt(r3(q), r3(kb), r3(vb), *lam_args, tq=tq, lam_init=lam_init)
    else:
        past = cache_k.shape[1]
        o = _attn_sample(r3(q), r3(kb), r3(vb), cache_k.reshape(bsz, past, -1),
                         cache_v.reshape(bsz, past, -1), *lam_args, lam_init=lam_init)
    x_mid = _merge(c.reshape(n, dc), o.reshape(n, -1), gates, p["w_conv_out"], p["w_attn_out"],
                   p["w_out"], x2d, tm=tm)
    y = _peer(x_mid, p["norm2_g"], p["peer_wq"], p["k1p"], p["k2p"], p["peer_u"], p["peer_v"],
              tm=tm_peer, ce=ce)
    return (y.reshape(bsz, t, d), k.reshape(bsz, t, N_HEADS, 2, D_HEAD),
            v.reshape(bsz, t, N_HEADS, D_VHEAD), conv_state)


def kernel(x_prompt, x_sample, cache_attn_k, cache_attn_v, state_conv, norm1_g, w_in, conv_dw_w, conv_dw_b,
           conv_ln_g, conv_ln_b, w_conv_out, q_norm_g, k_norm_g, lambda_q1, lambda_k1, lambda_q2, lambda_k2,
           subln_g, w_attn_out, w_out, norm2_g, peer_wq, peer_k1, peer_k2, peer_u, peer_v):
    depth = w_in.shape[0]
    xp, xs = x_prompt, x_sample
    outs = [[] for _ in range(6)]
    row = lambda a: a.reshape(1, -1)
    for l in range(depth):
        half = peer_k1.shape[-1]
        p = dict(
            norm1_g=row(norm1_g[l]), w_in=w_in[l].astype(BF16),
            conv_dw_w=conv_dw_w[l], conv_dw_b=row(conv_dw_b[l]),
            conv_ln_g=row(conv_ln_g[l]), conv_ln_b=row(conv_ln_b[l]),
            w_conv_out=w_conv_out[l].astype(BF16),
            qg=_tile128(q_norm_g[l]), kg=_tile128(k_norm_g[l]),
            lambda_q1=row(lambda_q1[l]), lambda_k1=row(lambda_k1[l]),
            lambda_q2=row(lambda_q2[l]), lambda_k2=row(lambda_k2[l]),
            subln_g=row(subln_g[l]),
            w_attn_out=w_attn_out[l].astype(BF16), w_out=w_out[l].astype(BF16),
            norm2_g=row(norm2_g[l]), peer_wq=peer_wq[l].astype(BF16),
            k1p=jnp.pad(peer_k1[l], ((0, 0), (0, LANES - half))),
            k2p=jnp.pad(peer_k2[l], ((0, 0), (LANES - half, 0))),
            peer_u=peer_u[l].astype(BF16), peer_v=peer_v[l].astype(BF16),
        )
        li = _lambda_init(l)
        bp, tp, _ = xp.shape
        bs, ts, _ = xs.shape
        zero_hist = jnp.zeros((bp, CONV_PAD, state_conv.shape[-1]), xp.dtype)
        xp, kp, vp, cp = _layer(xp, 0, zero_hist, None, None, p, li,
                                tm=min(512, bp * tp), tt=min(256, tp), tq=min(256, tp),
                                tm_peer=min(512, bp * tp), ce=512)
        hist_s = jnp.pad(state_conv[l], ((0, 0), (CONV_PAD - CONV_STATE, 0), (0, 0)))
        xs, ks, vs, cs = _layer(xs, cache_attn_k.shape[2], hist_s, cache_attn_k[l], cache_attn_v[l], p, li,
                                tm=min(256, bs * ts), tt=ts, tq=ts,
                                tm_peer=min(256, bs * ts), ce=512)
        for lst, val in zip(outs, (kp, vp, cp, ks, vs, cs)):
            lst.append(val)
    kp, vp, cp, ks, vs, cs = (jnp.stack(o) for o in outs)
    return (xp, xs, kp, vp, cp, ks, vs, cs)
```

```python
import functools
import math

import jax
import jax.numpy as jnp
from jax import lax
from jax.experimental import pallas as pl
from jax.experimental.pallas import tpu as pltpu

F32 = jnp.float32
BF16 = jnp.bfloat16
I32 = jnp.int32

LANES = 128
CHUNK = 64
EPS = 1e-6
N_HEADS = 8
D_HEAD = 64
D_VHEAD = 2 * D_HEAD
CONV_WIDTH = 31
CONV_STATE = CONV_WIDTH - 1
CONV_PAD = 32
ROPE_THETA = 10000.0
NEG_INF = -1e30
PEER_HEADS = 8
PEER_TOPK = 16
VMEM_LIMIT = 56 * 1024 * 1024

NT_DIMS = (((1,), (1,)), ((), ()))
TN_DIMS = (((0,), (0,)), ((), ()))


def _cparams(sem):
    return pltpu.CompilerParams(dimension_semantics=sem, vmem_limit_bytes=VMEM_LIMIT)


def _lambda_init(layer):
    return 0.8 - 0.6 * math.exp(-0.3 * layer)


def _qk_norm_rope(z, g, cos, sin_signed):
    tm = z.shape[0]
    lane = lax.broadcasted_iota(I32, (tm, LANES), 1)
    lo = lane < D_HEAD
    first = (lane & (D_HEAD // 2)) == 0
    outs = []
    for c in range(z.shape[1] // LANES):
        zc = z[:, c * LANES:(c + 1) * LANES]
        zz = zc * zc
        s_lo = jnp.sum(jnp.where(lo, zz, 0.0), axis=-1, keepdims=True)
        s_hi = jnp.sum(jnp.where(lo, 0.0, zz), axis=-1, keepdims=True)
        r = jnp.where(lo, lax.rsqrt(s_lo * (1.0 / D_HEAD) + EPS), lax.rsqrt(s_hi * (1.0 / D_HEAD) + EPS))
        y = zc * r * g
        up = pltpu.roll(y, LANES - D_HEAD // 2, 1)
        dn = pltpu.roll(y, D_HEAD // 2, 1)
        outs.append(y * cos + jnp.where(first, up, dn) * sin_signed)
    return jnp.concatenate(outs, axis=1)


def _in_proj_kernel(x_ref, g1_ref, w_ref, qg_ref, kg_ref,
                    glu_ref, q_ref, k_ref, kb_ref, v_ref, vb_ref, gate_ref,
                    h_scr, za_scr, cos_scr, sin_scr, *, seq_len, pos_offset, tm, k_transposed):
    i = pl.program_id(0)
    j = pl.program_id(1)

    @pl.when(j == 0)
    def _():
        x = x_ref[...]
        ms = jnp.mean(x * x, axis=-1, keepdims=True)
        h_scr[...] = (x * lax.rsqrt(ms + EPS) * g1_ref[...]).astype(BF16)
        row = lax.broadcasted_iota(I32, (tm, LANES), 0) + i * tm
        pos = (row & (seq_len - 1)) + pos_offset
        lane = lax.broadcasted_iota(I32, (tm, LANES), 1)
        f = (lane & (D_HEAD // 2 - 1)).astype(F32)
        inv = jnp.power(jnp.float32(ROPE_THETA), -f / (D_HEAD // 2))
        ang = pos.astype(F32) * inv
        cos_scr[...] = jnp.cos(ang)
        s = jnp.sin(ang)
        sin_scr[...] = jnp.where((lane & (D_HEAD // 2)) == 0, -s, s)

    z = jnp.dot(h_scr[...], w_ref[...], preferred_element_type=F32)

    @pl.when(j == 0)
    def _():
        za_scr[...] = z

    @pl.when(j == 1)
    def _():
        glu_ref[...] = za_scr[...] * jax.nn.sigmoid(z)

    @pl.when(j == 2)
    def _():
        q = _qk_norm_rope(z, qg_ref[...], cos_scr[...], sin_scr[...])
        q_ref[...] = (q * (D_HEAD ** -0.5)).astype(BF16)

    @pl.when(j == 3)
    def _():
        k = _qk_norm_rope(z, kg_ref[...], cos_scr[...], sin_scr[...])
        if k_transposed:
            kt = k.T
            k_ref[0] = kt
            kb_ref[0] = kt.astype(BF16)
        else:
            k_ref[...] = k
            kb_ref[...] = k.astype(BF16)

    @pl.when(j == 4)
    def _():
        v_ref[...] = z
        vb_ref[...] = z.astype(BF16)

    @pl.when(j >= 5)
    def _():
        gate_ref[...] = jax.nn.sigmoid(z).astype(BF16)


def _in_proj(x2d, g1, w_in_bf, qg, kg, *, seq_len, pos_offset, tm, k_transposed):
    n, d = x2d.shape
    d_in = w_in_bf.shape[1]
    tn = d // 2
    nj = d_in // tn
    assert n % tm == 0 and d_in % tn == 0 and nj == 9
    assert seq_len & (seq_len - 1) == 0
    blk = lambda: pl.BlockSpec((tm, tn), lambda i, j: (i, 0))
    if k_transposed:
        assert seq_len % tm == 0
        per = seq_len // tm
        k_shape = (n // seq_len, tn, seq_len)
        kblk = lambda: pl.BlockSpec((1, tn, tm), lambda i, j: (i // per, 0, i % per))
    else:
        k_shape = (n, tn)
        kblk = blk
    out_shape = (
        jax.ShapeDtypeStruct((n, tn), F32),
        jax.ShapeDtypeStruct((n, tn), BF16),
        jax.ShapeDtypeStruct(k_shape, F32),
        jax.ShapeDtypeStruct(k_shape, BF16),
        jax.ShapeDtypeStruct((n, tn), F32),
        jax.ShapeDtypeStruct((n, tn), BF16),
        jax.ShapeDtypeStruct((n, 4 * tn), BF16),
    )
    return pl.pallas_call(
        functools.partial(_in_proj_kernel, seq_len=seq_len, pos_offset=pos_offset, tm=tm,
                          k_transposed=k_transposed),
        grid=(n // tm, nj),
        in_specs=[
            pl.BlockSpec((tm, d), lambda i, j: (i, 0)),
            pl.BlockSpec((1, d), lambda i, j: (0, 0)),
            pl.BlockSpec((d, tn), lambda i, j: (0, j)),
            pl.BlockSpec((1, LANES), lambda i, j: (0, 0)),
            pl.BlockSpec((1, LANES), lambda i, j: (0, 0)),
        ],
        out_specs=(blk(), blk(), kblk(), kblk(), blk(), blk(),
                   pl.BlockSpec((tm, tn), lambda i, j: (i, jnp.maximum(j - 5, 0)))),
        out_shape=out_shape,
        scratch_shapes=[
            pltpu.VMEM((tm, d), BF16),
            pltpu.VMEM((tm, tn), F32),
            pltpu.VMEM((tm, LANES), F32),
            pltpu.VMEM((tm, LANES), F32),
        ],
        compiler_params=_cparams(("parallel", "arbitrary")),
        name="in_proj",
    )(x2d, g1, w_in_bf, qg, kg)


CONV_ROWS = 64


def _conv_kernel(cur_ref, prev_ref, st_ref, w_ref, b_ref, lg_ref, lb_ref,
                 c_ref, so_ref, buf, *, tt):
    ti = pl.program_id(1)
    dc = cur_ref.shape[2]

    @pl.when(ti == 0)
    def _():
        buf[0:CONV_PAD, :] = st_ref[0]

    @pl.when(ti > 0)
    def _():
        buf[0:CONV_PAD, :] = prev_ref[0]

    buf[CONV_PAD:CONV_PAD + tt, :] = cur_ref[0]
    lead = CONV_PAD - CONV_STATE
    rows = min(CONV_ROWS, tt)

    for rs in range(tt // rows):
        parts = []
        for cs in range(dc // LANES):
            sl = slice(cs * LANES, (cs + 1) * LANES)
            acc = jnp.zeros((rows, LANES), F32)
            for j in range(CONV_WIDTH):
                r0 = lead + rs * rows + j
                acc = acc + buf[r0:r0 + rows, sl] * w_ref[j:j + 1, sl]
            parts.append(acc + b_ref[:, sl])
        c = jnp.concatenate(parts, axis=1)
        mu = jnp.mean(c, axis=-1, keepdims=True)
        var = jnp.mean(jnp.square(c - mu), axis=-1, keepdims=True)
        y = (c - mu) * lax.rsqrt(var + EPS) * lg_ref[...] + lb_ref[...]
        c_ref[0, rs * rows:(rs + 1) * rows, :] = (y * jax.nn.sigmoid(y)).astype(BF16)

    @pl.when(ti == pl.num_programs(1) - 1)
    def _():
        so_ref[0] = buf[lead + tt:lead + tt + CONV_STATE, :]


def _conv(glu3d, hist, w, b, lg, lb, *, tt):
    bsz, t, dc = glu3d.shape
    assert t % tt == 0 and tt % min(CONV_ROWS, tt) == 0 and tt % CONV_PAD == 0
    per = tt // CONV_PAD
    return pl.pallas_call(
        functools.partial(_conv_kernel, tt=tt),
        grid=(bsz, t // tt),
        in_specs=[
            pl.BlockSpec((1, tt, dc), lambda bi, ti: (bi, ti, 0)),
            pl.BlockSpec((1, CONV_PAD, dc), lambda bi, ti: (bi, jnp.maximum(ti * per - 1, 0), 0)),
            pl.BlockSpec((1, CONV_PAD, dc), lambda bi, ti: (bi, 0, 0)),
            pl.BlockSpec((CONV_WIDTH, dc), lambda bi, ti: (0, 0)),
            pl.BlockSpec((1, dc), lambda bi, ti: (0, 0)),
            pl.BlockSpec((1, dc), lambda bi, ti: (0, 0)),
            pl.BlockSpec((1, dc), lambda bi, ti: (0, 0)),
        ],
        out_specs=(
            pl.BlockSpec((1, tt, dc), lambda bi, ti: (bi, ti, 0)),
            pl.BlockSpec((1, CONV_STATE, dc), lambda bi, ti: (bi, 0, 0)),
        ),
        out_shape=(
            jax.ShapeDtypeStruct((bsz, t, dc), BF16),
            jax.ShapeDtypeStruct((bsz, CONV_STATE, dc), F32),
        ),
        scratch_shapes=[pltpu.VMEM((CONV_PAD + tt, dc), F32)],
        compiler_params=_cparams(("parallel", "arbitrary")),
        name="conv_branch",
    )(glu3d, glu3d, hist, w, b, lg, lb)


def _lambda_value(lq1_ref, lk1_ref, lq2_ref, lk2_ref, lam_init):
    a = jnp.sum(lq1_ref[...] * lk1_ref[...], axis=-1, keepdims=True)
    b = jnp.sum(lq2_ref[...] * lk2_ref[...], axis=-1, keepdims=True)
    return jnp.exp(a) - jnp.exp(b) + lam_init


def _head_finish(o1, o2, lam, sg, lam_init):
    o = o1 - lam * o2
    ms = jnp.mean(o * o, axis=-1, keepdims=True)
    return (o * lax.rsqrt(ms + EPS) * sg) * (1.0 - lam_init)


def _split_heads(q):
    lane = lax.broadcasted_iota(I32, q.shape, 1)
    zero = jnp.zeros_like(q)
    return jnp.where(lane < D_HEAD, q, zero), jnp.where(lane >= D_HEAD, q, zero)


def _attn_prompt_kernel(qi_tab, ki_tab, q_ref, k_ref, v_ref, lq1_ref, lk1_ref, lq2_ref, lk2_ref, sg_ref,
                        o_ref, m_scr, acc_scr, *, tq, tk, lam_init):
    s_id = pl.program_id(2)
    qi = qi_tab[s_id]
    ki = ki_tab[s_id]

    @pl.when(ki == 0)
    def _():
        m_scr[...] = jnp.full(m_scr.shape, -jnp.inf, F32)
        acc_scr[...] = jnp.zeros(acc_scr.shape, F32)

    def step(diagonal):
        q = q_ref[0]
        kt = k_ref[0]
        v = v_ref[0]
        v_ext = jnp.concatenate([v, jnp.ones_like(v)], axis=1)
        if diagonal:
            row_chunk = lax.broadcasted_iota(I32, (tq, tk), 0) // CHUNK
            col_chunk = lax.broadcasted_iota(I32, (tq, tk), 1) // CHUNK
            mask = col_chunk <= row_chunk
        ss = [jnp.dot(qc, kt, preferred_element_type=F32) for qc in _split_heads(q)]
        if diagonal:
            ss = [jnp.where(mask, s, NEG_INF) for s in ss]
        m_prev = [m_scr[c] for c in range(2)]
        m_new = [jnp.maximum(m_prev[c], jnp.max(ss[c], axis=-1, keepdims=True)) for c in range(2)]
        ps = [jnp.exp(ss[c] - jnp.concatenate([m_new[c]] * (tk // LANES), axis=1)).astype(BF16) for c in range(2)]
        pvs = [jnp.dot(ps[c], v_ext, preferred_element_type=F32) for c in range(2)]
        for c in range(2):
            alpha = jnp.exp(m_prev[c] - m_new[c])
            acc_scr[c] = jnp.concatenate([alpha, alpha], axis=1) * acc_scr[c] + pvs[c]
            m_scr[c] = m_new[c]

    @pl.when(ki < qi)
    def _():
        step(False)

    @pl.when(ki == qi)
    def _():
        step(True)
        lam = _lambda_value(lq1_ref, lk1_ref, lq2_ref, lk2_ref, lam_init)
        a1 = acc_scr[0]
        a2 = acc_scr[1]
        y = _head_finish(a1[:, :D_VHEAD] / a1[:, D_VHEAD:], a2[:, :D_VHEAD] / a2[:, D_VHEAD:],
                         lam, sg_ref[...], lam_init)
        o_ref[0] = y.astype(BF16)


def _attn_prompt(q3, kt3, v3, lq1, lk1, lq2, lk2, sg, *, tq, lam_init):
    bsz, t, da = q3.shape
    nq = t // tq
    assert t % tq == 0 and tq % CHUNK == 0 and tq % LANES == 0
    pairs = [(a, b) for a in range(nq) for b in range(a + 1)]
    qi_tab = jnp.asarray([p[0] for p in pairs], I32)
    ki_tab = jnp.asarray([p[1] for p in pairs], I32)
    vec = lambda n: pl.BlockSpec((1, n), lambda b, h, s, qt, kt: (0, 0))
    grid_spec = pltpu.PrefetchScalarGridSpec(
        num_scalar_prefetch=2,
        grid=(bsz, N_HEADS, len(pairs)),
        in_specs=[
            pl.BlockSpec((1, tq, LANES), lambda b, h, s, qt, kt: (b, qt[s], h)),
            pl.BlockSpec((1, LANES, tq), lambda b, h, s, qt, kt: (b, h, kt[s])),
            pl.BlockSpec((1, tq, LANES), lambda b, h, s, qt, kt: (b, kt[s], h)),
            vec(D_HEAD), vec(D_HEAD), vec(D_HEAD), vec(D_HEAD), vec(D_VHEAD),
        ],
        out_specs=pl.BlockSpec((1, tq, LANES), lambda b, h, s, qt, kt: (b, qt[s], h)),
        scratch_shapes=[
            pltpu.VMEM((2, tq, LANES), F32),
            pltpu.VMEM((2, tq, 2 * D_VHEAD), F32),
        ],
    )
    return pl.pallas_call(
        functools.partial(_attn_prompt_kernel, tq=tq, tk=tq, lam_init=lam_init),
        grid_spec=grid_spec,
        out_shape=jax.ShapeDtypeStruct((bsz, t, da), BF16),
        compiler_params=_cparams(("parallel", "parallel", "arbitrary")),
        name="attn_prompt",
    )(qi_tab, ki_tab, q3, kt3, v3, lq1, lk1, lq2, lk2, sg)


def _attn_sample_kernel(q_ref, kn_ref, vn_ref, ck_ref, cv_ref, lq1_ref, lk1_ref, lq2_ref, lk2_ref, sg_ref,
                        o_ref, *, lam_init):
    q = q_ref[0]
    kn = kn_ref[0]
    vn = vn_ref[0]
    kpt = ck_ref[0].astype(BF16)
    vp = cv_ref[0].astype(BF16)
    outs = []
    for qc in _split_heads(q):
        sp = jnp.dot(qc, kpt, preferred_element_type=F32)
        sn = lax.dot_general(qc, kn, NT_DIMS, preferred_element_type=F32)
        m = jnp.maximum(jnp.max(sp, axis=-1, keepdims=True), jnp.max(sn, axis=-1, keepdims=True))
        pp = jnp.exp(sp - m)
        pn = jnp.exp(sn - m)
        l = jnp.sum(pp, axis=-1, keepdims=True) + jnp.sum(pn, axis=-1, keepdims=True)
        o = (jnp.dot(pp.astype(BF16), vp, preferred_element_type=F32)
             + jnp.dot(pn.astype(BF16), vn, preferred_element_type=F32))
        outs.append(o / l)
    lam = _lambda_value(lq1_ref, lk1_ref, lq2_ref, lk2_ref, lam_init)
    o_ref[0] = _head_finish(outs[0], outs[1], lam, sg_ref[...], lam_init).astype(BF16)


def _attn_sample(q3, kn3, vn3, ckt3, cv3, lq1, lk1, lq2, lk2, sg, *, lam_init):
    bsz, t, da = q3.shape
    past = cv3.shape[1]
    vec = lambda n: pl.BlockSpec((1, n), lambda b, h: (0, 0))
    new = lambda: pl.BlockSpec((1, t, LANES), lambda b, h: (b, 0, h))
    old = lambda: pl.BlockSpec((1, past, LANES), lambda b, h: (b, 0, h))
    old_t = lambda: pl.BlockSpec((1, LANES, past), lambda b, h: (b, h, 0))
    return pl.pallas_call(
        functools.partial(_attn_sample_kernel, lam_init=lam_init),
        grid=(bsz, N_HEADS),
        in_specs=[new(), new(), new(), old_t(), old(),
                  vec(D_HEAD), vec(D_HEAD), vec(D_HEAD), vec(D_HEAD), vec(D_VHEAD)],
        out_specs=new(),
        out_shape=jax.ShapeDtypeStruct((bsz, t, da), BF16),
        compiler_params=_cparams(("parallel", "parallel")),
        name="attn_sample",
    )(q3, kn3, vn3, ckt3, cv3, lq1, lk1, lq2, lk2, sg)


def _merge_kernel(c_ref, o_ref, gc_ref, ga_ref, wco_ref, wao_ref, wout_ref, x_ref, y_ref):
    j = pl.program_id(1)

    @pl.when(j == 0)
    def _():
        y_ref[...] = x_ref[...]

    co = jnp.dot(c_ref[...], wco_ref[...], preferred_element_type=F32)
    ao = jnp.dot(o_ref[...], wao_ref[...], preferred_element_type=F32)
    merged = gc_ref[...].astype(F32) * co + ga_ref[...].astype(F32) * ao
    y_ref[...] += jnp.dot(merged.astype(BF16), wout_ref[...], preferred_element_type=F32)


def _merge(c2d, o2d, gates, wco, wao, wout, x2d, *, tm):
    n, d = x2d.shape
    dh = d // 2
    assert n % tm == 0
    return pl.pallas_call(
        _merge_kernel,
        grid=(n // tm, 2),
        in_specs=[
            pl.BlockSpec((tm, dh), lambda i, j: (i, 0)),
            pl.BlockSpec((tm, dh), lambda i, j: (i, 0)),
            pl.BlockSpec((tm, dh), lambda i, j: (i, j)),
            pl.BlockSpec((tm, dh), lambda i, j: (i, 2 + j)),
            pl.BlockSpec((dh, dh), lambda i, j: (0, j)),
            pl.BlockSpec((dh, dh), lambda i, j: (0, j)),
            pl.BlockSpec((dh, d), lambda i, j: (j, 0)),
            pl.BlockSpec((tm, d), lambda i, j: (i, 0)),
        ],
        out_specs=pl.BlockSpec((tm, d), lambda i, j: (i, 0)),
        out_shape=jax.ShapeDtypeStruct((n, d), F32),
        compiler_params=_cparams(("parallel", "arbitrary")),
        name="merge_out_proj",
    )(c2d, o2d, gates, gates, wco, wao, wout, x2d)


def _topk_rows(cur_ref, rank_ref, val_ref, idx_ref, k):
    rows = cur_ref.shape[0]
    iota = lax.broadcasted_iota(I32, cur_ref.shape, 0).astype(F32)
    if rank_ref is not None:
        rank_ref[...] = jnp.full(rank_ref.shape, float(k), F32)

    def body(j, carry):
        cur = cur_ref[...]
        m = jnp.max(cur, axis=0, keepdims=True)
        idx = jnp.min(jnp.where(cur == m, iota, float(rows)), axis=0, keepdims=True)
        hit = iota == idx
        if rank_ref is not None:
            rank_ref[...] = jnp.where(hit, j.astype(F32), rank_ref[...])
        cur_ref[...] = jnp.where(hit, -jnp.inf, cur)
        val_ref[j] = m
        idx_ref[j] = idx
        return carry

    lax.fori_loop(0, k, body, 0)


def _cand_layout(kk):
    counts = [kk // (j1 + 1) for j1 in range(kk)]
    starts = [sum(counts[:j1]) for j1 in range(kk)]
    return counts, starts, sum(counts)


def _peer_route(x_ref, g2_ref, wq_ref, k1_ref, k2_ref, y_ref, h_scr, q_scr, e2_scr, rank2_scr, c_scr, n_scr,
                cur_scr, rank_scr, cand_scr, val1_scr, val2_scr, valc_scr, idx1_scr, idx_scr, *, tm):
    nk = k1_ref.shape[0]
    kk = PEER_TOPK
    counts, starts, ncand = _cand_layout(kk)
    x = x_ref[...]
    y_ref[...] = x
    ms = jnp.mean(x * x, axis=-1, keepdims=True)
    h = (x * lax.rsqrt(ms + EPS) * g2_ref[...]).astype(BF16)
    h_scr[...] = h
    q = jnp.dot(h, wq_ref[...], preferred_element_type=F32)
    for hd in range(PEER_HEADS):
        q_scr[hd] = q[:, hd * LANES:(hd + 1) * LANES]

    def head_body(hd, carry):
        qh = q_scr[hd]
        s1 = lax.dot_general(k1_ref[...], qh, NT_DIMS, preferred_element_type=F32)
        s2 = lax.dot_general(k2_ref[...], qh, NT_DIMS, preferred_element_type=F32)
        cur_scr[...] = s1
        _topk_rows(cur_scr, None, val1_scr, idx1_scr, kk)
        cur_scr[...] = s2
        _topk_rows(cur_scr, rank_scr, val2_scr, idx_scr, kk)
        rank2_scr[hd] = rank_scr[...].astype(BF16)
        v1max = val1_scr[0]
        v2max = val2_scr[0]
        v2all = jnp.concatenate([val2_scr[j2] for j2 in range(kk)], axis=0)
        pieces = [val1_scr[j1] + v2all[:counts[j1], :] for j1 in range(kk)]
        pieces.append(jnp.full((cand_scr.shape[0] - ncand, tm), -jnp.inf, F32))
        cand_scr[...] = jnp.concatenate(pieces, axis=0)
        _topk_rows(cand_scr, None, valc_scr, idx_scr, kk)
        top = valc_scr[0]
        z = jnp.zeros((1, tm), F32)
        row16 = lax.broadcasted_iota(I32, (kk, tm), 0).astype(F32)
        n16 = jnp.zeros((kk, tm), F32)
        for j in range(kk):
            z = z + jnp.exp(valc_scr[j] - top)
            row = idx_scr[j]
            j1 = jnp.zeros((1, tm), F32)
            for st in starts[1:]:
                j1 = j1 + jnp.where(row >= float(st), 1.0, 0.0)
            n16 = n16 + jnp.where(row16 == j1, 1.0, 0.0)
        iota = lax.broadcasted_iota(I32, (nk, tm), 0).astype(F32)
        n_by = jnp.zeros((nk, tm), F32)
        for j1 in range(kk):
            n_by = n_by + jnp.where(iota == idx1_scr[j1], n16[j1:j1 + 1, :], 0.0)
        n_scr[hd] = n_by
        c_scr[hd] = jnp.exp(s1 - v1max) / z
        e2_scr[hd] = jnp.exp(s2 - v2max).astype(BF16)
        return carry

    lax.fori_loop(0, PEER_HEADS, head_body, 0)


def _peer_scores(u_ref, h_scr, at_scr, slot):
    at_scr[slot] = lax.dot_general(u_ref[...], h_scr[...], NT_DIMS, preferred_element_type=F32)


def _peer_mix(chunk, slot, v_ref, y_ref, at_scr, e2_scr, rank2_scr, c_scr, n_scr, *, tm, ce, nk):
    per_dot = 2
    total = None
    for g in range(ce // (per_dot * nk)):
        parts = []
        for rr in range(per_dot):
            blk = g * per_dot + rr
            r = chunk * (ce // nk) + blk
            acc = jnp.zeros((nk, tm), BF16)
            for hd in range(PEER_HEADS):
                n_row = n_scr[hd, pl.ds(r, 1), :].astype(BF16)
                c_row = c_scr[hd, pl.ds(r, 1), :].astype(BF16)
                acc = acc + jnp.where(rank2_scr[hd] < n_row, e2_scr[hd] * c_row, 0)
            a = at_scr[slot, blk * nk:(blk + 1) * nk, :]
            gelu = 0.5 * a * (1.0 + lax.erf(a * (2.0 ** -0.5)))
            parts.append(acc * gelu.astype(BF16))
        wt = jnp.concatenate(parts, axis=0)
        rows = slice(g * per_dot * nk, (g + 1) * per_dot * nk)
        d = lax.dot_general(wt, v_ref[rows, :], TN_DIMS, preferred_element_type=F32)
        total = d if total is None else total + d
    y_ref[...] += total


def _peer_kernel(x_ref, g2_ref, wq_ref, k1_ref, k2_ref, u_ref, v_ref, y_ref,
                 h_scr, q_scr, at_scr, e2_scr, rank2_scr, c_scr, n_scr,
                 cur_scr, rank_scr, cand_scr, val1_scr, val2_scr, valc_scr, idx1_scr, idx_scr,
                 *, tm, ce):
    s = pl.program_id(1)
    last = pl.num_programs(1) - 1
    nk = k1_ref.shape[0]
    mix = functools.partial(_peer_mix, v_ref=v_ref, y_ref=y_ref, at_scr=at_scr, e2_scr=e2_scr,
                            rank2_scr=rank2_scr, c_scr=c_scr, n_scr=n_scr, tm=tm, ce=ce, nk=nk)

    @pl.when(s == 0)
    def _():
        _peer_route(x_ref, g2_ref, wq_ref, k1_ref, k2_ref, y_ref, h_scr, q_scr, e2_scr, rank2_scr, c_scr, n_scr,
                    cur_scr, rank_scr, cand_scr, val1_scr, val2_scr, valc_scr, idx1_scr, idx_scr, tm=tm)
        _peer_scores(u_ref, h_scr, at_scr, 0)

    @pl.when((s > 0) & (s < last))
    def _():
        _peer_scores(u_ref, h_scr, at_scr, s % 2)
        mix(s - 1, (s - 1) % 2)

    @pl.when(s == last)
    def _():
        mix(s - 1, (s - 1) % 2)


def _peer(x2d, g2, wq, k1p, k2p, u_bf, v_bf, *, tm, ce):
    n, d = x2d.shape
    ne = u_bf.shape[0]
    nk = k1p.shape[0]
    kk = PEER_TOPK
    nc = ne // ce
    ncand = -(-_cand_layout(kk)[2] // 8) * 8
    assert n % tm == 0 and ne % ce == 0 and ce % (2 * nk) == 0 and ne == nk * nk and nk == LANES
    return pl.pallas_call(
        functools.partial(_peer_kernel, tm=tm, ce=ce),
        grid=(n // tm, nc + 1),
        in_specs=[
            pl.BlockSpec((tm, d), lambda i, s: (i, 0)),
            pl.BlockSpec((1, d), lambda i, s: (0, 0)),
            pl.BlockSpec(wq.shape, lambda i, s: (0, 0)),
            pl.BlockSpec((nk, LANES), lambda i, s: (0, 0)),
            pl.BlockSpec((nk, LANES), lambda i, s: (0, 0)),
            pl.BlockSpec((ce, d), lambda i, s: (jnp.minimum(s, nc - 1), 0)),
            pl.BlockSpec((ce, d), lambda i, s: (jnp.maximum(s - 1, 0), 0)),
        ],
        out_specs=pl.BlockSpec((tm, d), lambda i, s: (i, 0)),
        out_shape=jax.ShapeDtypeStruct((n, d), F32),
        scratch_shapes=[
            pltpu.VMEM((tm, d), BF16),
            pltpu.VMEM((PEER_HEADS, tm, LANES), F32),
            pltpu.VMEM((2, ce, tm), F32),
            pltpu.VMEM((PEER_HEADS, nk, tm), BF16),
            pltpu.VMEM((PEER_HEADS, nk, tm), BF16),
            pltpu.VMEM((PEER_HEADS, nk, tm), F32),
            pltpu.VMEM((PEER_HEADS, nk, tm), F32),
            pltpu.VMEM((nk, tm), F32),
            pltpu.VMEM((nk, tm), F32),
            pltpu.VMEM((ncand, tm), F32),
            pltpu.VMEM((kk, 1, tm), F32),
            pltpu.VMEM((kk, 1, tm), F32),
            pltpu.VMEM((kk, 1, tm), F32),
            pltpu.VMEM((kk, 1, tm), F32),
            pltpu.VMEM((kk, 1, tm), F32),
        ],
        compiler_params=_cparams(("parallel", "arbitrary")),
        name="peer",
    )(x2d, g2, wq, k1p, k2p, u_bf, v_bf)


def _tile128(g):
    return jnp.tile(g.reshape(1, -1), (1, LANES // g.shape[-1]))


def _layer(x3, pos_offset, hist, cache_k, cache_v, p, lam_init, *, tm, tt, tq, tm_peer, ce):
    bsz, t, d = x3.shape
    n = bsz * t
    x2d = x3.reshape(n, d)
    prompt = cache_k is None
    glu, q, k, kb, v, vb, gates = _in_proj(x2d, p["norm1_g"], p["w_in"], p["qg"], p["kg"],
                                           seq_len=t, pos_offset=pos_offset, tm=tm, k_transposed=prompt)
    dc = glu.shape[1]
    c, conv_state = _conv(glu.reshape(bsz, t, dc), hist, p["conv_dw_w"], p["conv_dw_b"],
                          p["conv_ln_g"], p["conv_ln_b"], tt=tt)
    lam_args = (p["lambda_q1"], p["lambda_k1"], p["lambda_q2"], p["lambda_k2"], p["subln_g"])
    r3 = lambda a: a.reshape(bsz, t, -1)
    if prompt:
        o = _attn_prompt(r3(q), kb, r3(vb), *lam_args, tq=tq, lam_init=lam_init)
        k = jnp.transpose(k.reshape(bsz, N_HEADS, 2, D_HEAD, t), (0, 4, 1, 2, 3))
    else:
        past = cache_k.shape[1]
        ckt = jnp.transpose(cache_k, (0, 2, 3, 4, 1)).reshape(bsz, -1, past)
        o = _attn_sample(r3(q), r3(kb), r3(vb), ckt, cache_v.reshape(bsz, past, -1), *lam_args,
                         lam_init=lam_init)
        k = k.reshape(bsz, t, N_HEADS, 2, D_HEAD)
    x_mid = _merge(c.reshape(n, dc), o.reshape(n, -1), gates, p["w_conv_out"], p["w_attn_out"],
                   p["w_out"], x2d, tm=tm)
    y = _peer(x_mid, p["norm2_g"], p["peer_wq"], p["k1p"], p["k2p"], p["peer_u"], p["peer_v"],
              tm=tm_peer, ce=ce)
    return (y.reshape(bsz, t, d), k, v.reshape(bsz, t, N_HEADS, D_VHEAD), conv_state)


def kernel(x_prompt, x_sample, cache_attn_k, cache_attn_v, state_conv, norm1_g, w_in, conv_dw_w, conv_dw_b,
           conv_ln_g, conv_ln_b, w_conv_out, q_norm_g, k_norm_g, lambda_q1, lambda_k1, lambda_q2, lambda_k2,
           subln_g, w_attn_out, w_out, norm2_g, peer_wq, peer_k1, peer_k2, peer_u, peer_v):
    depth = w_in.shape[0]
    xp, xs = x_prompt, x_sample
    outs = [[] for _ in range(6)]
    row = lambda a: a.reshape(1, -1)
    for l in range(depth):
        half = peer_k1.shape[-1]
        p = dict(
            norm1_g=row(norm1_g[l]), w_in=w_in[l].astype(BF16),
            conv_dw_w=conv_dw_w[l], conv_dw_b=row(conv_dw_b[l]),
            conv_ln_g=row(conv_ln_g[l]), conv_ln_b=row(conv_ln_b[l]),
            w_conv_out=w_conv_out[l].astype(BF16),
            qg=_tile128(q_norm_g[l]), kg=_tile128(k_norm_g[l]),
            lambda_q1=row(lambda_q1[l]), lambda_k1=row(lambda_k1[l]),
            lambda_q2=row(lambda_q2[l]), lambda_k2=row(lambda_k2[l]),
            subln_g=row(subln_g[l]),
            w_attn_out=w_attn_out[l].astype(BF16), w_out=w_out[l].astype(BF16),
            norm2_g=row(norm2_g[l]), peer_wq=peer_wq[l].astype(BF16),
            k1p=jnp.pad(peer_k1[l], ((0, 0), (0, LANES - half))),
            k2p=jnp.pad(peer_k2[l], ((0, 0), (LANES - half, 0))),
            peer_u=peer_u[l].astype(BF16), peer_v=peer_v[l].astype(BF16),
        )
        li = _lambda_init(l)
        bp, tp, _ = xp.shape
        bs, ts, _ = xs.shape
        zero_hist = jnp.zeros((bp, CONV_PAD, state_conv.shape[-1]), xp.dtype)
        xp, kp, vp, cp = _layer(xp, 0, zero_hist, None, None, p, li,
                                tm=min(512, bp * tp), tt=min(256, tp), tq=min(512, tp),
                                tm_peer=min(512, bp * tp), ce=512)
        hist_s = jnp.pad(state_conv[l], ((0, 0), (CONV_PAD - CONV_STATE, 0), (0, 0)))
        xs, ks, vs, cs = _layer(xs, cache_attn_k.shape[2], hist_s, cache_attn_k[l], cache_attn_v[l], p, li,
                                tm=min(256, bs * ts), tt=ts, tq=ts,
                                tm_peer=min(256, bs * ts), ce=512)
        for lst, val in zip(outs, (kp, vp, cp, ks, vs, cs)):
            lst.append(val)
    kp, vp, cp, ks, vs, cs = (jnp.stack(o) for o in outs)
    return (xp, xs, kp, vp, cp, ks, vs, cs)
```

```python
import functools
import math

import jax
import jax.numpy as jnp
from jax import lax
from jax.experimental import pallas as pl
from jax.experimental.pallas import tpu as pltpu

F32 = jnp.float32
BF16 = jnp.bfloat16
I32 = jnp.int32

LANES = 128
CHUNK = 64
EPS = 1e-6
N_HEADS = 8
D_HEAD = 64
D_VHEAD = 2 * D_HEAD
CONV_WIDTH = 31
CONV_STATE = CONV_WIDTH - 1
CONV_PAD = 32
ROPE_THETA = 10000.0
NEG_INF = -1e30
PEER_HEADS = 8
PEER_TOPK = 16
VMEM_LIMIT = 56 * 1024 * 1024

NT_DIMS = (((1,), (1,)), ((), ()))
TN_DIMS = (((0,), (0,)), ((), ()))


def _cparams(sem):
    return pltpu.CompilerParams(dimension_semantics=sem, vmem_limit_bytes=VMEM_LIMIT)


def _lambda_init(layer):
    return 0.8 - 0.6 * math.exp(-0.3 * layer)


def _qk_norm_rope(z, g, cos, sin_signed):
    tm = z.shape[0]
    lane = lax.broadcasted_iota(I32, (tm, LANES), 1)
    lo = lane < D_HEAD
    first = (lane & (D_HEAD // 2)) == 0
    outs = []
    for c in range(z.shape[1] // LANES):
        zc = z[:, c * LANES:(c + 1) * LANES]
        zz = zc * zc
        s_lo = jnp.sum(jnp.where(lo, zz, 0.0), axis=-1, keepdims=True)
        s_hi = jnp.sum(jnp.where(lo, 0.0, zz), axis=-1, keepdims=True)
        r = jnp.where(lo, lax.rsqrt(s_lo * (1.0 / D_HEAD) + EPS), lax.rsqrt(s_hi * (1.0 / D_HEAD) + EPS))
        y = zc * r * g
        up = pltpu.roll(y, LANES - D_HEAD // 2, 1)
        dn = pltpu.roll(y, D_HEAD // 2, 1)
        outs.append(y * cos + jnp.where(first, up, dn) * sin_signed)
    return jnp.concatenate(outs, axis=1)


def _in_proj_kernel(x_ref, g1_ref, w_ref, qg_ref, kg_ref,
                    glu_ref, q_ref, k_ref, kb_ref, v_ref, vb_ref, gate_ref,
                    h_scr, za_scr, z0_scr, z1_scr, cos_scr, sin_scr,
                    *, seq_len, pos_offset, tm, k_transposed, n_blocks):
    i = pl.program_id(0)
    j = pl.program_id(1)
    z_scr = (z0_scr, z1_scr)

    def prologue():
        x = x_ref[...]
        ms = jnp.mean(x * x, axis=-1, keepdims=True)
        h_scr[...] = (x * lax.rsqrt(ms + EPS) * g1_ref[...]).astype(BF16)
        row = lax.broadcasted_iota(I32, (tm, LANES), 0) + i * tm
        pos = (row & (seq_len - 1)) + pos_offset
        lane = lax.broadcasted_iota(I32, (tm, LANES), 1)
        f = (lane & (D_HEAD // 2 - 1)).astype(F32)
        inv = jnp.power(jnp.float32(ROPE_THETA), -f / (D_HEAD // 2))
        ang = pos.astype(F32) * inv
        cos_scr[...] = jnp.cos(ang)
        s = jnp.sin(ang)
        sin_scr[...] = jnp.where((lane & (D_HEAD // 2)) == 0, -s, s)

    def finish(blk, z):
        if blk == 0:
            za_scr[...] = z
        elif blk == 1:
            glu_ref[...] = za_scr[...] * jax.nn.sigmoid(z)
        elif blk == 2:
            q = _qk_norm_rope(z, qg_ref[...], cos_scr[...], sin_scr[...])
            q_ref[...] = (q * (D_HEAD ** -0.5)).astype(BF16)
        elif blk == 3:
            k = _qk_norm_rope(z, kg_ref[...], cos_scr[...], sin_scr[...])
            if k_transposed:
                kt = k.T
                k_ref[0] = kt
                kb_ref[0] = kt.astype(BF16)
            else:
                k_ref[...] = k
                kb_ref[...] = k.astype(BF16)
        elif blk == 4:
            v_ref[...] = z
            vb_ref[...] = z.astype(BF16)
        else:
            gate_ref[...] = jax.nn.sigmoid(z).astype(BF16)

    for step in range(n_blocks + 1):
        @pl.when(j == step)
        def _(step=step):
            if step == 0:
                prologue()
            if step >= 1:
                finish(step - 1, z_scr[(step - 1) % 2][...])
            if step < n_blocks:
                z_scr[step % 2][...] = jnp.dot(h_scr[...], w_ref[...], preferred_element_type=F32)


def _in_proj(x2d, g1, w_in_bf, qg, kg, *, seq_len, pos_offset, tm, k_transposed):
    n, d = x2d.shape
    d_in = w_in_bf.shape[1]
    tn = d // 2
    nj = d_in // tn
    assert n % tm == 0 and d_in % tn == 0 and nj == 9
    assert seq_len & (seq_len - 1) == 0
    blk = lambda: pl.BlockSpec((tm, tn), lambda i, j: (i, 0))
    if k_transposed:
        assert seq_len % tm == 0
        per = seq_len // tm
        k_shape = (n // seq_len, tn, seq_len)
        kblk = lambda: pl.BlockSpec((1, tn, tm), lambda i, j: (i // per, 0, i % per))
    else:
        k_shape = (n, tn)
        kblk = blk
    out_shape = (
        jax.ShapeDtypeStruct((n, tn), F32),
        jax.ShapeDtypeStruct((n, tn), BF16),
        jax.ShapeDtypeStruct(k_shape, F32),
        jax.ShapeDtypeStruct(k_shape, BF16),
        jax.ShapeDtypeStruct((n, tn), F32),
        jax.ShapeDtypeStruct((n, tn), BF16),
        jax.ShapeDtypeStruct((n, 4 * tn), BF16),
    )
    return pl.pallas_call(
        functools.partial(_in_proj_kernel, seq_len=seq_len, pos_offset=pos_offset, tm=tm,
                          k_transposed=k_transposed, n_blocks=nj),
        grid=(n // tm, nj + 1),
        in_specs=[
            pl.BlockSpec((tm, d), lambda i, j: (i, 0)),
            pl.BlockSpec((1, d), lambda i, j: (0, 0)),
            pl.BlockSpec((d, tn), lambda i, j: (0, jnp.minimum(j, nj - 1))),
            pl.BlockSpec((1, LANES), lambda i, j: (0, 0)),
            pl.BlockSpec((1, LANES), lambda i, j: (0, 0)),
        ],
        out_specs=(blk(), blk(), kblk(), kblk(), blk(), blk(),
                   pl.BlockSpec((tm, tn), lambda i, j: (i, jnp.maximum(j - 6, 0)))),
        out_shape=out_shape,
        scratch_shapes=[
            pltpu.VMEM((tm, d), BF16),
            pltpu.VMEM((tm, tn), F32),
            pltpu.VMEM((tm, tn), F32),
            pltpu.VMEM((tm, tn), F32),
            pltpu.VMEM((tm, LANES), F32),
            pltpu.VMEM((tm, LANES), F32),
        ],
        compiler_params=_cparams(("parallel", "arbitrary")),
        name="in_proj",
    )(x2d, g1, w_in_bf, qg, kg)


CONV_ROWS = 64


def _conv_kernel(cur_ref, prev_ref, st_ref, w_ref, b_ref, lg_ref, lb_ref,
                 c_ref, so_ref, buf, *, tt):
    ti = pl.program_id(1)
    dc = cur_ref.shape[2]

    @pl.when(ti == 0)
    def _():
        buf[0:CONV_PAD, :] = st_ref[0]

    @pl.when(ti > 0)
    def _():
        buf[0:CONV_PAD, :] = prev_ref[0]

    buf[CONV_PAD:CONV_PAD + tt, :] = cur_ref[0]
    lead = CONV_PAD - CONV_STATE
    rows = min(CONV_ROWS, tt)

    for rs in range(tt // rows):
        parts = []
        for cs in range(dc // LANES):
            sl = slice(cs * LANES, (cs + 1) * LANES)
            acc = jnp.zeros((rows, LANES), F32)
            for j in range(CONV_WIDTH):
                r0 = lead + rs * rows + j
                acc = acc + buf[r0:r0 + rows, sl] * w_ref[j:j + 1, sl]
            parts.append(acc + b_ref[:, sl])
        c = jnp.concatenate(parts, axis=1)
        mu = jnp.mean(c, axis=-1, keepdims=True)
        var = jnp.mean(jnp.square(c - mu), axis=-1, keepdims=True)
        y = (c - mu) * lax.rsqrt(var + EPS) * lg_ref[...] + lb_ref[...]
        c_ref[0, rs * rows:(rs + 1) * rows, :] = (y * jax.nn.sigmoid(y)).astype(BF16)

    @pl.when(ti == pl.num_programs(1) - 1)
    def _():
        so_ref[0] = buf[lead + tt:lead + tt + CONV_STATE, :]


def _conv(glu3d, hist, w, b, lg, lb, *, tt):
    bsz, t, dc = glu3d.shape
    assert t % tt == 0 and tt % min(CONV_ROWS, tt) == 0 and tt % CONV_PAD == 0
    per = tt // CONV_PAD
    return pl.pallas_call(
        functools.partial(_conv_kernel, tt=tt),
        grid=(bsz, t // tt),
        in_specs=[
            pl.BlockSpec((1, tt, dc), lambda bi, ti: (bi, ti, 0)),
            pl.BlockSpec((1, CONV_PAD, dc), lambda bi, ti: (bi, jnp.maximum(ti * per - 1, 0), 0)),
            pl.BlockSpec((1, CONV_PAD, dc), lambda bi, ti: (bi, 0, 0)),
            pl.BlockSpec((CONV_WIDTH, dc), lambda bi, ti: (0, 0)),
            pl.BlockSpec((1, dc), lambda bi, ti: (0, 0)),
            pl.BlockSpec((1, dc), lambda bi, ti: (0, 0)),
            pl.BlockSpec((1, dc), lambda bi, ti: (0, 0)),
        ],
        out_specs=(
            pl.BlockSpec((1, tt, dc), lambda bi, ti: (bi, ti, 0)),
            pl.BlockSpec((1, CONV_STATE, dc), lambda bi, ti: (bi, 0, 0)),
        ),
        out_shape=(
            jax.ShapeDtypeStruct((bsz, t, dc), BF16),
            jax.ShapeDtypeStruct((bsz, CONV_STATE, dc), F32),
        ),
        scratch_shapes=[pltpu.VMEM((CONV_PAD + tt, dc), F32)],
        compiler_params=_cparams(("parallel", "arbitrary")),
        name="conv_branch",
    )(glu3d, glu3d, hist, w, b, lg, lb)


def _lambda_value(lq1_ref, lk1_ref, lq2_ref, lk2_ref, lam_init):
    a = jnp.sum(lq1_ref[...] * lk1_ref[...], axis=-1, keepdims=True)
    b = jnp.sum(lq2_ref[...] * lk2_ref[...], axis=-1, keepdims=True)
    return jnp.exp(a) - jnp.exp(b) + lam_init


def _head_finish(o1, o2, lam, sg, lam_init):
    o = o1 - lam * o2
    ms = jnp.mean(o * o, axis=-1, keepdims=True)
    return (o * lax.rsqrt(ms + EPS) * sg) * (1.0 - lam_init)


def _split_heads(q):
    lane = lax.broadcasted_iota(I32, q.shape, 1)
    zero = jnp.zeros_like(q)
    return jnp.where(lane < D_HEAD, q, zero), jnp.where(lane >= D_HEAD, q, zero)


def _attn_prompt_kernel(qi_tab, ki_tab, q_ref, k_ref, v_ref, lq1_ref, lk1_ref, lq2_ref, lk2_ref, sg_ref,
                        o_ref, m_scr, acc_scr, *, tq, tk, lam_init):
    s_id = pl.program_id(2)
    qi = qi_tab[s_id]
    ki = ki_tab[s_id]

    @pl.when(ki == 0)
    def _():
        m_scr[...] = jnp.full(m_scr.shape, -jnp.inf, F32)
        acc_scr[...] = jnp.zeros(acc_scr.shape, F32)

    def step(diagonal):
        q = q_ref[0]
        kt = k_ref[0]
        v = v_ref[0]
        v_ext = jnp.concatenate([v, jnp.ones_like(v)], axis=1)
        if diagonal:
            row_chunk = lax.broadcasted_iota(I32, (tq, tk), 0) // CHUNK
            col_chunk = lax.broadcasted_iota(I32, (tq, tk), 1) // CHUNK
            mask = col_chunk <= row_chunk
        ss = [jnp.dot(qc, kt, preferred_element_type=F32) for qc in _split_heads(q)]
        if diagonal:
            ss = [jnp.where(mask, s, NEG_INF) for s in ss]
        m_prev = [m_scr[c] for c in range(2)]
        m_new = [jnp.maximum(m_prev[c], jnp.max(ss[c], axis=-1, keepdims=True)) for c in range(2)]
        ps = [jnp.exp(ss[c] - jnp.concatenate([m_new[c]] * (tk // LANES), axis=1)).astype(BF16) for c in range(2)]
        pvs = [jnp.dot(ps[c], v_ext, preferred_element_type=F32) for c in range(2)]
        for c in range(2):
            alpha = jnp.exp(m_prev[c] - m_new[c])
            acc_scr[c] = jnp.concatenate([alpha, alpha], axis=1) * acc_scr[c] + pvs[c]
            m_scr[c] = m_new[c]

    @pl.when(ki < qi)
    def _():
        step(False)

    @pl.when(ki == qi)
    def _():
        step(True)
        lam = _lambda_value(lq1_ref, lk1_ref, lq2_ref, lk2_ref, lam_init)
        a1 = acc_scr[0]
        a2 = acc_scr[1]
        y = _head_finish(a1[:, :D_VHEAD] / a1[:, D_VHEAD:], a2[:, :D_VHEAD] / a2[:, D_VHEAD:],
                         lam, sg_ref[...], lam_init)
        o_ref[0] = y.astype(BF16)


def _attn_prompt(q3, kt3, v3, lq1, lk1, lq2, lk2, sg, *, tq, lam_init):
    bsz, t, da = q3.shape
    nq = t // tq
    assert t % tq == 0 and tq % CHUNK == 0 and tq % LANES == 0
    pairs = [(a, b) for a in range(nq) for b in range(a + 1)]
    qi_tab = jnp.asarray([p[0] for p in pairs], I32)
    ki_tab = jnp.asarray([p[1] for p in pairs], I32)
    vec = lambda n: pl.BlockSpec((1, n), lambda b, h, s, qt, kt: (0, 0))
    grid_spec = pltpu.PrefetchScalarGridSpec(
        num_scalar_prefetch=2,
        grid=(bsz, N_HEADS, len(pairs)),
        in_specs=[
            pl.BlockSpec((1, tq, LANES), lambda b, h, s, qt, kt: (b, qt[s], h)),
            pl.BlockSpec((1, LANES, tq), lambda b, h, s, qt, kt: (b, h, kt[s])),
            pl.BlockSpec((1, tq, LANES), lambda b, h, s, qt, kt: (b, kt[s], h)),
            vec(D_HEAD), vec(D_HEAD), vec(D_HEAD), vec(D_HEAD), vec(D_VHEAD),
        ],
        out_specs=pl.BlockSpec((1, tq, LANES), lambda b, h, s, qt, kt: (b, qt[s], h)),
        scratch_shapes=[
            pltpu.VMEM((2, tq, LANES), F32),
            pltpu.VMEM((2, tq, 2 * D_VHEAD), F32),
        ],
    )
    return pl.pallas_call(
        functools.partial(_attn_prompt_kernel, tq=tq, tk=tq, lam_init=lam_init),
        grid_spec=grid_spec,
        out_shape=jax.ShapeDtypeStruct((bsz, t, da), BF16),
        compiler_params=_cparams(("parallel", "parallel", "arbitrary")),
        name="attn_prompt",
    )(qi_tab, ki_tab, q3, kt3, v3, lq1, lk1, lq2, lk2, sg)


def _attn_sample_kernel(q_ref, kn_ref, vn_ref, ck_ref, cv_ref, lq1_ref, lk1_ref, lq2_ref, lk2_ref, sg_ref,
                        o_ref, *, lam_init):
    q = q_ref[0]
    kn = kn_ref[0]
    vn = vn_ref[0]
    kpt = ck_ref[0].astype(BF16)
    vp = cv_ref[0].astype(BF16)
    outs = []
    for qc in _split_heads(q):
        sp = jnp.dot(qc, kpt, preferred_element_type=F32)
        sn = lax.dot_general(qc, kn, NT_DIMS, preferred_element_type=F32)
        m = jnp.maximum(jnp.max(sp, axis=-1, keepdims=True), jnp.max(sn, axis=-1, keepdims=True))
        pp = jnp.exp(sp - m)
        pn = jnp.exp(sn - m)
        l = jnp.sum(pp, axis=-1, keepdims=True) + jnp.sum(pn, axis=-1, keepdims=True)
        o = (jnp.dot(pp.astype(BF16), vp, preferred_element_type=F32)
             + jnp.dot(pn.astype(BF16), vn, preferred_element_type=F32))
        outs.append(o / l)
    lam = _lambda_value(lq1_ref, lk1_ref, lq2_ref, lk2_ref, lam_init)
    o_ref[0] = _head_finish(outs[0], outs[1], lam, sg_ref[...], lam_init).astype(BF16)


def _attn_sample(q3, kn3, vn3, ckt3, cv3, lq1, lk1, lq2, lk2, sg, *, lam_init):
    bsz, t, da = q3.shape
    past = cv3.shape[1]
    vec = lambda n: pl.BlockSpec((1, n), lambda b, h: (0, 0))
    new = lambda: pl.BlockSpec((1, t, LANES), lambda b, h: (b, 0, h))
    old = lambda: pl.BlockSpec((1, past, LANES), lambda b, h: (b, 0, h))
    old_t = lambda: pl.BlockSpec((1, LANES, past), lambda b, h: (b, h, 0))
    return pl.pallas_call(
        functools.partial(_attn_sample_kernel, lam_init=lam_init),
        grid=(bsz, N_HEADS),
        in_specs=[new(), new(), new(), old_t(), old(),
                  vec(D_HEAD), vec(D_HEAD), vec(D_HEAD), vec(D_HEAD), vec(D_VHEAD)],
        out_specs=new(),
        out_shape=jax.ShapeDtypeStruct((bsz, t, da), BF16),
        compiler_params=_cparams(("parallel", "parallel")),
        name="attn_sample",
    )(q3, kn3, vn3, ckt3, cv3, lq1, lk1, lq2, lk2, sg)


def _merge_kernel(c_ref, o_ref, gc_ref, ga_ref, wco_ref, wao_ref, wout_ref, x_ref, y_ref):
    j = pl.program_id(1)

    @pl.when(j == 0)
    def _():
        y_ref[...] = x_ref[...]

    co = jnp.dot(c_ref[...], wco_ref[...], preferred_element_type=F32)
    ao = jnp.dot(o_ref[...], wao_ref[...], preferred_element_type=F32)
    merged = gc_ref[...].astype(F32) * co + ga_ref[...].astype(F32) * ao
    y_ref[...] += jnp.dot(merged.astype(BF16), wout_ref[...], preferred_element_type=F32)


def _merge(c2d, o2d, gates, wco, wao, wout, x2d, *, tm):
    n, d = x2d.shape
    dh = d // 2
    assert n % tm == 0
    return pl.pallas_call(
        _merge_kernel,
        grid=(n // tm, 2),
        in_specs=[
            pl.BlockSpec((tm, dh), lambda i, j: (i, 0)),
            pl.BlockSpec((tm, dh), lambda i, j: (i, 0)),
            pl.BlockSpec((tm, dh), lambda i, j: (i, j)),
            pl.BlockSpec((tm, dh), lambda i, j: (i, 2 + j)),
            pl.BlockSpec((dh, dh), lambda i, j: (0, j)),
            pl.BlockSpec((dh, dh), lambda i, j: (0, j)),
            pl.BlockSpec((dh, d), lambda i, j: (j, 0)),
            pl.BlockSpec((tm, d), lambda i, j: (i, 0)),
        ],
        out_specs=pl.BlockSpec((tm, d), lambda i, j: (i, 0)),
        out_shape=jax.ShapeDtypeStruct((n, d), F32),
        compiler_params=_cparams(("parallel", "arbitrary")),
        name="merge_out_proj",
    )(c2d, o2d, gates, gates, wco, wao, wout, x2d)


def _topk_rows(cur_ref, rank_ref, val_ref, idx_ref, k):
    rows = cur_ref.shape[0]
    iota = lax.broadcasted_iota(I32, cur_ref.shape, 0).astype(F32)
    if rank_ref is not None:
        rank_ref[...] = jnp.full(rank_ref.shape, float(k), F32)

    def body(j, carry):
        cur = cur_ref[...]
        m = jnp.max(cur, axis=0, keepdims=True)
        idx = jnp.min(jnp.where(cur == m, iota, float(rows)), axis=0, keepdims=True)
        hit = iota == idx
        if rank_ref is not None:
            rank_ref[...] = jnp.where(hit, lax.convert_element_type(j, F32), rank_ref[...])
        cur_ref[...] = jnp.where(hit, -jnp.inf, cur)
        val_ref[j] = m
        idx_ref[j] = idx
        return carry

    lax.fori_loop(0, k, body, 0)


def _cand_layout(kk):
    counts = [kk // (j1 + 1) for j1 in range(kk)]
    starts = [sum(counts[:j1]) for j1 in range(kk)]
    return counts, starts, sum(counts)


def _peer_route(x_ref, g2_ref, wq_ref, k1_ref, k2_ref, yt_scr, ht_scr, q_scr, e2_scr, rank2_scr, c_scr, n_scr,
                cur_scr, rank_scr, cand_scr, val1_scr, val2_scr, valc_scr, idx1_scr, idx_scr, *, tm):
    nk = k1_ref.shape[0]
    kk = PEER_TOPK
    counts, starts, ncand = _cand_layout(kk)
    x = x_ref[...]
    yt_scr[...] = jnp.zeros(yt_scr.shape, F32)
    ms = jnp.mean(x * x, axis=-1, keepdims=True)
    hf = x * lax.rsqrt(ms + EPS) * g2_ref[...]
    ht_scr[...] = hf.T.astype(BF16)
    q = jnp.dot(hf.astype(BF16), wq_ref[...], preferred_element_type=F32)
    for hd in range(PEER_HEADS):
        q_scr[hd] = q[:, hd * LANES:(hd + 1) * LANES]

    def head_body(hd, carry):
        qh = q_scr[hd]
        s1 = lax.dot_general(k1_ref[...], qh, NT_DIMS, preferred_element_type=F32)
        s2 = lax.dot_general(k2_ref[...], qh, NT_DIMS, preferred_element_type=F32)
        cur_scr[...] = s1
        _topk_rows(cur_scr, None, val1_scr, idx1_scr, kk)
        cur_scr[...] = s2
        _topk_rows(cur_scr, rank_scr, val2_scr, idx_scr, kk)
        rank2_scr[hd] = rank_scr[...].astype(BF16)
        v1max = val1_scr[0]
        v2max = val2_scr[0]
        v2all = jnp.concatenate([val2_scr[j2] for j2 in range(kk)], axis=0)
        pieces = [val1_scr[j1] + v2all[:counts[j1], :] for j1 in range(kk)]
        pieces.append(jnp.full((cand_scr.shape[0] - ncand, tm), -jnp.inf, F32))
        cand_scr[...] = jnp.concatenate(pieces, axis=0)
        _topk_rows(cand_scr, None, valc_scr, idx_scr, kk)
        top = valc_scr[0]
        z = jnp.zeros((1, tm), F32)
        row16 = lax.broadcasted_iota(I32, (kk, tm), 0).astype(F32)
        n16 = jnp.zeros((kk, tm), F32)
        for j in range(kk):
            z = z + jnp.exp(valc_scr[j] - top)
            row = idx_scr[j]
            j1 = jnp.zeros((1, tm), F32)
            for st in starts[1:]:
                j1 = j1 + jnp.where(row >= float(st), 1.0, 0.0)
            n16 = n16 + jnp.where(row16 == j1, 1.0, 0.0)
        iota = lax.broadcasted_iota(I32, (nk, tm), 0).astype(F32)
        n_by = jnp.zeros((nk, tm), F32)
        for j1 in range(kk):
            n_by = n_by + jnp.where(iota == idx1_scr[j1], n16[j1:j1 + 1, :], 0.0)
        n_scr[hd] = n_by
        c_scr[hd] = jnp.exp(s1 - v1max) / z
        e2_scr[hd] = jnp.exp(s2 - v2max).astype(BF16)
        return carry

    lax.fori_loop(0, PEER_HEADS, head_body, 0)


def _peer_scores(u_ref, ht_scr, at_scr):
    at_scr[...] = jnp.dot(u_ref[...], ht_scr[...], preferred_element_type=F32)


PEER_KEYS_PER_DOT = 2


def _peer_gates(chunk, at_scr, wt_scr, e2_scr, rank2_scr, c_scr, n_scr, *, tm, ce, nk):
    for blk in range(ce // nk):
        r = chunk * (ce // nk) + blk
        acc = jnp.zeros((nk, tm), BF16)
        for hd in range(PEER_HEADS):
            n_row = n_scr[hd, pl.ds(r, 1), :].astype(BF16)
            c_row = c_scr[hd, pl.ds(r, 1), :].astype(BF16)
            acc = acc + jnp.where(rank2_scr[hd] < n_row, e2_scr[hd] * c_row, 0)
        a = at_scr[blk * nk:(blk + 1) * nk, :]
        gelu = 0.5 * a * (1.0 + lax.erf(a * (2.0 ** -0.5)))
        wt_scr[blk * nk:(blk + 1) * nk, :] = acc * gelu.astype(BF16)


def _peer_combine(wt_scr, vt_ref, yt_scr, *, nk):
    per = PEER_KEYS_PER_DOT * nk
    total = None
    for g in range(wt_scr.shape[0] // per):
        rows = slice(g * per, (g + 1) * per)
        d = jnp.dot(vt_ref[:, rows], wt_scr[rows, :], preferred_element_type=F32)
        total = d if total is None else total + d
    yt_scr[...] += total


def _peer_kernel(x_ref, g2_ref, wq_ref, k1_ref, k2_ref, u_ref, vt_ref, y_ref,
                 ht_scr, yt_scr, q_scr, at0_scr, at1_scr, wt_scr, e2_scr, rank2_scr, c_scr, n_scr,
                 cur_scr, rank_scr, cand_scr, val1_scr, val2_scr, valc_scr, idx1_scr, idx_scr,
                 *, tm, ce):
    s = pl.program_id(1)
    last = pl.num_programs(1) - 1
    nk = k1_ref.shape[0]
    even = s % 2 == 0

    def step(score_into, mix_from):
        if mix_from is not None:
            _peer_gates(s - 1, mix_from, wt_scr, e2_scr, rank2_scr, c_scr, n_scr, tm=tm, ce=ce, nk=nk)
        if score_into is not None:
            _peer_scores(u_ref, ht_scr, score_into)
        if mix_from is not None:
            _peer_combine(wt_scr, vt_ref, yt_scr, nk=nk)

    @pl.when(s == 0)
    def _():
        _peer_route(x_ref, g2_ref, wq_ref, k1_ref, k2_ref, yt_scr, ht_scr, q_scr, e2_scr, rank2_scr, c_scr, n_scr,
                    cur_scr, rank_scr, cand_scr, val1_scr, val2_scr, valc_scr, idx1_scr, idx_scr, tm=tm)
        step(at0_scr, None)

    @pl.when((s > 0) & (s < last) & even)
    def _():
        step(at0_scr, at1_scr)

    @pl.when((s < last) & jnp.logical_not(even))
    def _():
        step(at1_scr, at0_scr)

    @pl.when(s == last)
    def _():
        step(None, at1_scr)
        y_ref[...] = x_ref[...] + yt_scr[...].T


def _peer(x2d, g2, wq, k1p, k2p, u_bf, vt_bf, *, tm, ce):
    n, d = x2d.shape
    ne = u_bf.shape[0]
    nk = k1p.shape[0]
    kk = PEER_TOPK
    nc = ne // ce
    ncand = -(-_cand_layout(kk)[2] // 8) * 8
    assert n % tm == 0 and ne % ce == 0 and ce % (2 * nk) == 0 and ne == nk * nk and nk == LANES
    assert nc % 2 == 0
    return pl.pallas_call(
        functools.partial(_peer_kernel, tm=tm, ce=ce),
        grid=(n // tm, nc + 1),
        in_specs=[
            pl.BlockSpec((tm, d), lambda i, s: (i, 0)),
            pl.BlockSpec((1, d), lambda i, s: (0, 0)),
            pl.BlockSpec(wq.shape, lambda i, s: (0, 0)),
            pl.BlockSpec((nk, LANES), lambda i, s: (0, 0)),
            pl.BlockSpec((nk, LANES), lambda i, s: (0, 0)),
            pl.BlockSpec((ce, d), lambda i, s: (jnp.minimum(s, nc - 1), 0)),
            pl.BlockSpec((d, ce), lambda i, s: (0, jnp.maximum(s - 1, 0))),
        ],
        out_specs=pl.BlockSpec((tm, d), lambda i, s: (i, 0)),
        out_shape=jax.ShapeDtypeStruct((n, d), F32),
        scratch_shapes=[
            pltpu.VMEM((d, tm), BF16),
            pltpu.VMEM((d, tm), F32),
            pltpu.VMEM((PEER_HEADS, tm, LANES), F32),
            pltpu.VMEM((ce, tm), F32),
            pltpu.VMEM((ce, tm), F32),
            pltpu.VMEM((ce, tm), BF16),
            pltpu.VMEM((PEER_HEADS, nk, tm), BF16),
            pltpu.VMEM((PEER_HEADS, nk, tm), BF16),
            pltpu.VMEM((PEER_HEADS, nk, tm), F32),
            pltpu.VMEM((PEER_HEADS, nk, tm), F32),
            pltpu.VMEM((nk, tm), F32),
            pltpu.VMEM((nk, tm), F32),
            pltpu.VMEM((ncand, tm), F32),
            pltpu.VMEM((kk, 1, tm), F32),
            pltpu.VMEM((kk, 1, tm), F32),
            pltpu.VMEM((kk, 1, tm), F32),
            pltpu.VMEM((kk, 1, tm), F32),
            pltpu.VMEM((kk, 1, tm), F32),
        ],
        compiler_params=_cparams(("parallel", "arbitrary")),
        name="peer",
    )(x2d, g2, wq, k1p, k2p, u_bf, vt_bf)


def _tile128(g):
    return jnp.tile(g.reshape(1, -1), (1, LANES // g.shape[-1]))


def _layer(x3, pos_offset, hist, cache_k, cache_v, p, lam_init, *, tm, tt, tq, tm_peer, ce):
    bsz, t, d = x3.shape
    n = bsz * t
    x2d = x3.reshape(n, d)
    prompt = cache_k is None
    glu, q, k, kb, v, vb, gates = _in_proj(x2d, p["norm1_g"], p["w_in"], p["qg"], p["kg"],
                                           seq_len=t, pos_offset=pos_offset, tm=tm, k_transposed=prompt)
    dc = glu.shape[1]
    c, conv_state = _conv(glu.reshape(bsz, t, dc), hist, p["conv_dw_w"], p["conv_dw_b"],
                          p["conv_ln_g"], p["conv_ln_b"], tt=tt)
    lam_args = (p["lambda_q1"], p["lambda_k1"], p["lambda_q2"], p["lambda_k2"], p["subln_g"])
    r3 = lambda a: a.reshape(bsz, t, -1)
    if prompt:
        o = _attn_prompt(r3(q), kb, r3(vb), *lam_args, tq=tq, lam_init=lam_init)
        k = jnp.transpose(k.reshape(bsz, N_HEADS, 2, D_HEAD, t), (0, 4, 1, 2, 3))
    else:
        past = cache_k.shape[1]
        ckt = jnp.transpose(cache_k, (0, 2, 3, 4, 1)).reshape(bsz, -1, past)
        o = _attn_sample(r3(q), r3(kb), r3(vb), ckt, cache_v.reshape(bsz, past, -1), *lam_args,
                         lam_init=lam_init)
        k = k.reshape(bsz, t, N_HEADS, 2, D_HEAD)
    x_mid = _merge(c.reshape(n, dc), o.reshape(n, -1), gates, p["w_conv_out"], p["w_attn_out"],
                   p["w_out"], x2d, tm=tm)
    y = _peer(x_mid, p["norm2_g"], p["peer_wq"], p["k1p"], p["k2p"], p["peer_u"], p["peer_vt"],
              tm=tm_peer, ce=ce)
    return (y.reshape(bsz, t, d), k, v.reshape(bsz, t, N_HEADS, D_VHEAD), conv_state)


def kernel(x_prompt, x_sample, cache_attn_k, cache_attn_v, state_conv, norm1_g, w_in, conv_dw_w, conv_dw_b,
           conv_ln_g, conv_ln_b, w_conv_out, q_norm_g, k_norm_g, lambda_q1, lambda_k1, lambda_q2, lambda_k2,
           subln_g, w_attn_out, w_out, norm2_g, peer_wq, peer_k1, peer_k2, peer_u, peer_v):
    depth = w_in.shape[0]
    xp, xs = x_prompt, x_sample
    outs = [[] for _ in range(6)]
    row = lambda a: a.reshape(1, -1)
    for l in range(depth):
        half = peer_k1.shape[-1]
        p = dict(
            norm1_g=row(norm1_g[l]), w_in=w_in[l].astype(BF16),
            conv_dw_w=conv_dw_w[l], conv_dw_b=row(conv_dw_b[l]),
            conv_ln_g=row(conv_ln_g[l]), conv_ln_b=row(conv_ln_b[l]),
            w_conv_out=w_conv_out[l].astype(BF16),
            qg=_tile128(q_norm_g[l]), kg=_tile128(k_norm_g[l]),
            lambda_q1=row(lambda_q1[l]), lambda_k1=row(lambda_k1[l]),
            lambda_q2=row(lambda_q2[l]), lambda_k2=row(lambda_k2[l]),
            subln_g=row(subln_g[l]),
            w_attn_out=w_attn_out[l].astype(BF16), w_out=w_out[l].astype(BF16),
            norm2_g=row(norm2_g[l]), peer_wq=peer_wq[l].astype(BF16),
            k1p=jnp.pad(peer_k1[l], ((0, 0), (0, LANES - half))),
            k2p=jnp.pad(peer_k2[l], ((0, 0), (LANES - half, 0))),
            peer_u=peer_u[l].astype(BF16), peer_vt=peer_v[l].astype(BF16).T,
        )
        li = _lambda_init(l)
        bp, tp, _ = xp.shape
        bs, ts, _ = xs.shape
        zero_hist = jnp.zeros((bp, CONV_PAD, state_conv.shape[-1]), xp.dtype)
        xp, kp, vp, cp = _layer(xp, 0, zero_hist, None, None, p, li,
                                tm=min(512, bp * tp), tt=min(256, tp), tq=min(512, tp),
                                tm_peer=min(512, bp * tp), ce=512)
        hist_s = jnp.pad(state_conv[l], ((0, 0), (CONV_PAD - CONV_STATE, 0), (0, 0)))
        xs, ks, vs, cs = _layer(xs, cache_attn_k.shape[2], hist_s, cache_attn_k[l], cache_attn_v[l], p, li,
                                tm=min(256, bs * ts), tt=ts, tq=ts,
                                tm_peer=min(256, bs * ts), ce=512)
        for lst, val in zip(outs, (kp, vp, cp, ks, vs, cs)):
            lst.append(val)
    kp, vp, cp, ks, vs, cs = (jnp.stack(o) for o in outs)
    return (xp, xs, kp, vp, cp, ks, vs, cs)
```

```python
import functools
import math

import jax
import jax.numpy as jnp
from jax import lax
from jax.experimental import pallas as pl
from jax.experimental.pallas import tpu as pltpu

F32 = jnp.float32
BF16 = jnp.bfloat16
I32 = jnp.int32

LANES = 128
SUBLANES = 8
CHUNK = 64
EPS = 1e-6
N_HEADS = 8
D_HEAD = 64
D_VHEAD = 2 * D_HEAD
CONV_WIDTH = 31
CONV_STATE = CONV_WIDTH - 1
CONV_PAD = 32
ROPE_THETA = 10000.0
NEG_INF = -1e30
PEER_HEADS = 8
PEER_TOPK = 16
VMEM_LIMIT = 56 * 1024 * 1024

NT_DIMS = (((1,), (1,)), ((), ()))
TN_DIMS = (((0,), (0,)), ((), ()))


def _cparams(sem):
    return pltpu.CompilerParams(dimension_semantics=sem, vmem_limit_bytes=VMEM_LIMIT)


def _lambda_init(layer):
    return 0.8 - 0.6 * math.exp(-0.3 * layer)


def _qk_norm_rope(z, g, cos, sin_signed):
    tm = z.shape[0]
    lane = lax.broadcasted_iota(I32, (tm, LANES), 1)
    lo = lane < D_HEAD
    first = (lane & (D_HEAD // 2)) == 0
    outs = []
    for c in range(z.shape[1] // LANES):
        zc = z[:, c * LANES:(c + 1) * LANES]
        zz = zc * zc
        s_lo = jnp.sum(jnp.where(lo, zz, 0.0), axis=-1, keepdims=True)
        s_hi = jnp.sum(jnp.where(lo, 0.0, zz), axis=-1, keepdims=True)
        r = jnp.where(lo, lax.rsqrt(s_lo * (1.0 / D_HEAD) + EPS), lax.rsqrt(s_hi * (1.0 / D_HEAD) + EPS))
        y = zc * r * g
        up = pltpu.roll(y, LANES - D_HEAD // 2, 1)
        dn = pltpu.roll(y, D_HEAD // 2, 1)
        outs.append(y * cos + jnp.where(first, up, dn) * sin_signed)
    return jnp.concatenate(outs, axis=1)


def _in_proj_kernel(x_ref, g1_ref, w_ref, qg_ref, kg_ref,
                    glu_ref, q_ref, k_ref, kb_ref, v_ref, vb_ref, gate_ref,
                    h_scr, za_scr, z0_scr, z1_scr, cos_scr, sin_scr,
                    *, seq_len, pos_offset, tm, k_transposed, n_blocks):
    i = pl.program_id(0)
    j = pl.program_id(1)
    z_scr = (z0_scr, z1_scr)

    def prologue():
        x = x_ref[...]
        ms = jnp.mean(x * x, axis=-1, keepdims=True)
        h_scr[...] = (x * lax.rsqrt(ms + EPS) * g1_ref[...]).astype(BF16)
        row = lax.broadcasted_iota(I32, (tm, LANES), 0) + i * tm
        pos = (row & (seq_len - 1)) + pos_offset
        lane = lax.broadcasted_iota(I32, (tm, LANES), 1)
        f = (lane & (D_HEAD // 2 - 1)).astype(F32)
        inv = jnp.power(jnp.float32(ROPE_THETA), -f / (D_HEAD // 2))
        ang = pos.astype(F32) * inv
        cos_scr[...] = jnp.cos(ang)
        s = jnp.sin(ang)
        sin_scr[...] = jnp.where((lane & (D_HEAD // 2)) == 0, -s, s)

    def finish(blk, z):
        if blk == 0:
            za_scr[...] = z
        elif blk == 1:
            glu_ref[...] = za_scr[...] * jax.nn.sigmoid(z)
        elif blk == 2:
            q = _qk_norm_rope(z, qg_ref[...], cos_scr[...], sin_scr[...])
            q_ref[...] = (q * (D_HEAD ** -0.5)).astype(BF16)
        elif blk == 3:
            k = _qk_norm_rope(z, kg_ref[...], cos_scr[...], sin_scr[...])
            if k_transposed:
                kt = k.T
                k_ref[0] = kt
                kb_ref[0] = kt.astype(BF16)
            else:
                k_ref[...] = k
                kb_ref[...] = k.astype(BF16)
        elif blk == 4:
            v_ref[...] = z
            vb_ref[...] = z.astype(BF16)
        else:
            gate_ref[...] = jax.nn.sigmoid(z).astype(BF16)

    for step in range(n_blocks + 1):
        @pl.when(j == step)
        def _(step=step):
            if step == 0:
                prologue()
            if step >= 1:
                finish(step - 1, z_scr[(step - 1) % 2][...])
            if step < n_blocks:
                z_scr[step % 2][...] = jnp.dot(h_scr[...], w_ref[...], preferred_element_type=F32)


def _in_proj(x2d, g1, w_in_bf, qg, kg, *, seq_len, pos_offset, tm, k_transposed):
    n, d = x2d.shape
    d_in = w_in_bf.shape[1]
    tn = d // 2
    nj = d_in // tn
    assert n % tm == 0 and d_in % tn == 0 and nj == 9
    assert seq_len & (seq_len - 1) == 0
    blk = lambda: pl.BlockSpec((tm, tn), lambda i, j: (i, 0))
    if k_transposed:
        assert seq_len % tm == 0
        per = seq_len // tm
        k_shape = (n // seq_len, tn, seq_len)
        kblk = lambda: pl.BlockSpec((1, tn, tm), lambda i, j: (i // per, 0, i % per))
    else:
        k_shape = (n, tn)
        kblk = blk
    out_shape = (
        jax.ShapeDtypeStruct((n, tn), F32),
        jax.ShapeDtypeStruct((n, tn), BF16),
        jax.ShapeDtypeStruct(k_shape, F32),
        jax.ShapeDtypeStruct(k_shape, BF16),
        jax.ShapeDtypeStruct((n, tn), F32),
        jax.ShapeDtypeStruct((n, tn), BF16),
        jax.ShapeDtypeStruct((n, 4 * tn), BF16),
    )
    return pl.pallas_call(
        functools.partial(_in_proj_kernel, seq_len=seq_len, pos_offset=pos_offset, tm=tm,
                          k_transposed=k_transposed, n_blocks=nj),
        grid=(n // tm, nj + 1),
        in_specs=[
            pl.BlockSpec((tm, d), lambda i, j: (i, 0)),
            pl.BlockSpec((1, d), lambda i, j: (0, 0)),
            pl.BlockSpec((d, tn), lambda i, j: (0, jnp.minimum(j, nj - 1))),
            pl.BlockSpec((1, LANES), lambda i, j: (0, 0)),
            pl.BlockSpec((1, LANES), lambda i, j: (0, 0)),
        ],
        out_specs=(blk(), blk(), kblk(), kblk(), blk(), blk(),
                   pl.BlockSpec((tm, tn), lambda i, j: (i, jnp.maximum(j - 6, 0)))),
        out_shape=out_shape,
        scratch_shapes=[
            pltpu.VMEM((tm, d), BF16),
            pltpu.VMEM((tm, tn), F32),
            pltpu.VMEM((tm, tn), F32),
            pltpu.VMEM((tm, tn), F32),
            pltpu.VMEM((tm, LANES), F32),
            pltpu.VMEM((tm, LANES), F32),
        ],
        compiler_params=_cparams(("parallel", "arbitrary")),
        name="in_proj",
    )(x2d, g1, w_in_bf, qg, kg)


CONV_ROWS = 64


def _conv_kernel(cur_ref, prev_ref, st_ref, w_ref, b_ref, lg_ref, lb_ref,
                 c_ref, so_ref, buf, shifted, *, tt):
    ti = pl.program_id(1)
    dc = cur_ref.shape[2]

    @pl.when(ti == 0)
    def _():
        buf[0:CONV_PAD, :] = st_ref[0]

    @pl.when(ti > 0)
    def _():
        buf[0:CONV_PAD, :] = prev_ref[0]

    buf[CONV_PAD:CONV_PAD + tt, :] = cur_ref[0]
    lead = CONV_PAD - CONV_STATE
    rows = min(CONV_ROWS, tt)
    span = shifted.shape[1]
    for p in range(1, SUBLANES):
        shifted[p - 1] = buf[p:p + span, :]

    for rs in range(tt // rows):
        parts = []
        for cs in range(dc // LANES):
            sl = slice(cs * LANES, (cs + 1) * LANES)
            acc = jnp.zeros((rows, LANES), F32)
            for j in range(CONV_WIDTH):
                phase = (lead + j) % SUBLANES
                r0 = lead + j - phase + rs * rows
                src = buf if phase == 0 else shifted.at[phase - 1]
                acc = acc + src[r0:r0 + rows, sl] * w_ref[j:j + 1, sl]
            parts.append(acc + b_ref[:, sl])
        c = jnp.concatenate(parts, axis=1)
        mu = jnp.mean(c, axis=-1, keepdims=True)
        var = jnp.mean(jnp.square(c - mu), axis=-1, keepdims=True)
        y = (c - mu) * lax.rsqrt(var + EPS) * lg_ref[...] + lb_ref[...]
        c_ref[0, rs * rows:(rs + 1) * rows, :] = (y * jax.nn.sigmoid(y)).astype(BF16)

    @pl.when(ti == pl.num_programs(1) - 1)
    def _():
        so_ref[0] = buf[lead + tt:lead + tt + CONV_STATE, :]


def _conv(glu3d, hist, w, b, lg, lb, *, tt):
    bsz, t, dc = glu3d.shape
    assert t % tt == 0 and tt % min(CONV_ROWS, tt) == 0 and tt % CONV_PAD == 0
    per = tt // CONV_PAD
    return pl.pallas_call(
        functools.partial(_conv_kernel, tt=tt),
        grid=(bsz, t // tt),
        in_specs=[
            pl.BlockSpec((1, tt, dc), lambda bi, ti: (bi, ti, 0)),
            pl.BlockSpec((1, CONV_PAD, dc), lambda bi, ti: (bi, jnp.maximum(ti * per - 1, 0), 0)),
            pl.BlockSpec((1, CONV_PAD, dc), lambda bi, ti: (bi, 0, 0)),
            pl.BlockSpec((CONV_WIDTH, dc), lambda bi, ti: (0, 0)),
            pl.BlockSpec((1, dc), lambda bi, ti: (0, 0)),
            pl.BlockSpec((1, dc), lambda bi, ti: (0, 0)),
            pl.BlockSpec((1, dc), lambda bi, ti: (0, 0)),
        ],
        out_specs=(
            pl.BlockSpec((1, tt, dc), lambda bi, ti: (bi, ti, 0)),
            pl.BlockSpec((1, CONV_STATE, dc), lambda bi, ti: (bi, 0, 0)),
        ),
        out_shape=(
            jax.ShapeDtypeStruct((bsz, t, dc), BF16),
            jax.ShapeDtypeStruct((bsz, CONV_STATE, dc), F32),
        ),
        scratch_shapes=[
            pltpu.VMEM((CONV_PAD + tt, dc), F32),
            pltpu.VMEM((SUBLANES - 1, tt + CONV_PAD - SUBLANES, dc), F32),
        ],
        compiler_params=_cparams(("parallel", "arbitrary")),
        name="conv_branch",
    )(glu3d, glu3d, hist, w, b, lg, lb)


def _lambda_value(lq1_ref, lk1_ref, lq2_ref, lk2_ref, lam_init):
    a = jnp.sum(lq1_ref[...] * lk1_ref[...], axis=-1, keepdims=True)
    b = jnp.sum(lq2_ref[...] * lk2_ref[...], axis=-1, keepdims=True)
    return jnp.exp(a) - jnp.exp(b) + lam_init


def _head_finish(o1, o2, lam, sg, lam_init):
    o = o1 - lam * o2
    ms = jnp.mean(o * o, axis=-1, keepdims=True)
    return (o * lax.rsqrt(ms + EPS) * sg) * (1.0 - lam_init)


def _split_heads(q):
    lane = lax.broadcasted_iota(I32, q.shape, 1)
    zero = jnp.zeros_like(q)
    return jnp.where(lane < D_HEAD, q, zero), jnp.where(lane >= D_HEAD, q, zero)


ATTN_FULL_FULL, ATTN_FULL_DIAG, ATTN_DIAG = 0, 1, 2


def _attn_blocks(qcs, blocks, state):
    m, acc = state
    scores = []
    for kt, _, diagonal in blocks:
        ss = [jnp.dot(qc, kt, preferred_element_type=F32) for qc in qcs]
        if diagonal:
            tq, tk = ss[0].shape
            row_chunk = lax.broadcasted_iota(I32, (tq, tk), 0) // CHUNK
            col_chunk = lax.broadcasted_iota(I32, (tq, tk), 1) // CHUNK
            mask = col_chunk <= row_chunk
            ss = [jnp.where(mask, s, NEG_INF) for s in ss]
        scores.append(ss)
    for ss, (_, v, _) in zip(scores, blocks):
        tk = ss[0].shape[1]
        v_ext = jnp.concatenate([v, jnp.ones_like(v)], axis=1)
        m_new = [jnp.maximum(m[c], jnp.max(ss[c], axis=-1, keepdims=True)) for c in range(2)]
        ps = [jnp.exp(ss[c] - jnp.concatenate([m_new[c]] * (tk // LANES), axis=1)).astype(BF16) for c in range(2)]
        pvs = [jnp.dot(ps[c], v_ext, preferred_element_type=F32) for c in range(2)]
        alphas = [jnp.exp(m[c] - m_new[c]) for c in range(2)]
        acc = [jnp.concatenate([alphas[c], alphas[c]], axis=1) * acc[c] + pvs[c] for c in range(2)]
        m = m_new
    return m, acc


def _attn_prompt_kernel(qi_tab, ka_tab, kb_tab, kind_tab, q_ref, kta_ref, ktb_ref, va_ref, vb_ref,
                        lq1_ref, lk1_ref, lq2_ref, lk2_ref, sg_ref, o_ref, m_scr, acc_scr, *, lam_init):
    s_id = pl.program_id(2)
    kind = kind_tab[s_id]

    @pl.when(ka_tab[s_id] == 0)
    def _():
        m_scr[...] = jnp.full(m_scr.shape, -jnp.inf, F32)
        acc_scr[...] = jnp.zeros(acc_scr.shape, F32)

    def load():
        return _split_heads(q_ref[0]), ([m_scr[c] for c in range(2)], [acc_scr[c] for c in range(2)])

    def store(state):
        for c in range(2):
            m_scr[c] = state[0][c]
            acc_scr[c] = state[1][c]

    def finish(state):
        lam = _lambda_value(lq1_ref, lk1_ref, lq2_ref, lk2_ref, lam_init)
        a1, a2 = state[1]
        y = _head_finish(a1[:, :D_VHEAD] / a1[:, D_VHEAD:], a2[:, :D_VHEAD] / a2[:, D_VHEAD:],
                         lam, sg_ref[...], lam_init)
        o_ref[0] = y.astype(BF16)

    @pl.when(kind == ATTN_FULL_FULL)
    def _():
        qcs, state = load()
        store(_attn_blocks(qcs, [(kta_ref[0], va_ref[0], False), (ktb_ref[0], vb_ref[0], False)], state))

    @pl.when(kind == ATTN_FULL_DIAG)
    def _():
        qcs, state = load()
        finish(_attn_blocks(qcs, [(kta_ref[0], va_ref[0], False), (ktb_ref[0], vb_ref[0], True)], state))

    @pl.when(kind == ATTN_DIAG)
    def _():
        qcs, state = load()
        finish(_attn_blocks(qcs, [(kta_ref[0], va_ref[0], True)], state))


def _attn_steps(nq):
    steps = []
    for qi in range(nq):
        ki = 0
        while ki + 1 < qi:
            steps.append((qi, ki, ki + 1, ATTN_FULL_FULL))
            ki += 2
        if ki + 1 == qi:
            steps.append((qi, ki, qi, ATTN_FULL_DIAG))
        else:
            steps.append((qi, qi, qi, ATTN_DIAG))
    return steps


def _attn_prompt(q3, kt3, v3, lq1, lk1, lq2, lk2, sg, *, tq, lam_init):
    bsz, t, da = q3.shape
    nq = t // tq
    assert t % tq == 0 and tq % CHUNK == 0 and tq % LANES == 0
    steps = _attn_steps(nq)
    tabs = [jnp.asarray([st[i] for st in steps], I32) for i in range(4)]
    vec = lambda n: pl.BlockSpec((1, n), lambda b, h, s, *_: (0, 0))
    grid_spec = pltpu.PrefetchScalarGridSpec(
        num_scalar_prefetch=4,
        grid=(bsz, N_HEADS, len(steps)),
        in_specs=[
            pl.BlockSpec((1, tq, LANES), lambda b, h, s, qt, ka, kb, kd: (b, qt[s], h)),
            pl.BlockSpec((1, LANES, tq), lambda b, h, s, qt, ka, kb, kd: (b, h, ka[s])),
            pl.BlockSpec((1, LANES, tq), lambda b, h, s, qt, ka, kb, kd: (b, h, kb[s])),
            pl.BlockSpec((1, tq, LANES), lambda b, h, s, qt, ka, kb, kd: (b, ka[s], h)),
            pl.BlockSpec((1, tq, LANES), lambda b, h, s, qt, ka, kb, kd: (b, kb[s], h)),
            vec(D_HEAD), vec(D_HEAD), vec(D_HEAD), vec(D_HEAD), vec(D_VHEAD),
        ],
        out_specs=pl.BlockSpec((1, tq, LANES), lambda b, h, s, qt, ka, kb, kd: (b, qt[s], h)),
        scratch_shapes=[
            pltpu.VMEM((2, tq, LANES), F32),
            pltpu.VMEM((2, tq, 2 * D_VHEAD), F32),
        ],
    )
    return pl.pallas_call(
        functools.partial(_attn_prompt_kernel, lam_init=lam_init),
        grid_spec=grid_spec,
        out_shape=jax.ShapeDtypeStruct((bsz, t, da), BF16),
        compiler_params=_cparams(("parallel", "parallel", "arbitrary")),
        name="attn_prompt",
    )(*tabs, q3, kt3, kt3, v3, v3, lq1, lk1, lq2, lk2, sg)


def _attn_sample_kernel(q_ref, kn_ref, vn_ref, ck_ref, cv_ref, lq1_ref, lk1_ref, lq2_ref, lk2_ref, sg_ref,
                        o_ref, *, lam_init):
    past = ck_ref.shape[2]
    lam = _lambda_value(lq1_ref, lk1_ref, lq2_ref, lk2_ref, lam_init)
    for h in range(N_HEADS):
        sl = slice(h * LANES, (h + 1) * LANES)
        q = q_ref[0, :, sl]
        kn = kn_ref[0, :, sl]
        vn = vn_ref[0, :, sl]
        kpt = ck_ref[0, sl, :].astype(BF16)
        vp = cv_ref[0, pl.ds(h, past, stride=N_HEADS), :].astype(BF16)
        outs = []
        for qc in _split_heads(q):
            sp = jnp.dot(qc, kpt, preferred_element_type=F32)
            sn = lax.dot_general(qc, kn, NT_DIMS, preferred_element_type=F32)
            m = jnp.maximum(jnp.max(sp, axis=-1, keepdims=True), jnp.max(sn, axis=-1, keepdims=True))
            pp = jnp.exp(sp - m)
            pn = jnp.exp(sn - m)
            l = jnp.sum(pp, axis=-1, keepdims=True) + jnp.sum(pn, axis=-1, keepdims=True)
            o = (jnp.dot(pp.astype(BF16), vp, preferred_element_type=F32)
                 + jnp.dot(pn.astype(BF16), vn, preferred_element_type=F32))
            outs.append(o / l)
        o_ref[0, :, sl] = _head_finish(outs[0], outs[1], lam, sg_ref[...], lam_init).astype(BF16)


def _attn_sample(q3, kn3, vn3, ckt3, cv3, lq1, lk1, lq2, lk2, sg, *, lam_init):
    bsz, t, da = q3.shape
    past = ckt3.shape[2]
    vec = lambda n: pl.BlockSpec((1, n), lambda b: (0, 0))
    new = lambda: pl.BlockSpec((1, t, da), lambda b: (b, 0, 0))
    return pl.pallas_call(
        functools.partial(_attn_sample_kernel, lam_init=lam_init),
        grid=(bsz,),
        in_specs=[new(), new(), new(),
                  pl.BlockSpec((1, da, past), lambda b: (b, 0, 0)),
                  pl.BlockSpec((1, past * N_HEADS, D_VHEAD), lambda b: (b, 0, 0)),
                  vec(D_HEAD), vec(D_HEAD), vec(D_HEAD), vec(D_HEAD), vec(D_VHEAD)],
        out_specs=new(),
        out_shape=jax.ShapeDtypeStruct((bsz, t, da), BF16),
        compiler_params=_cparams(("parallel",)),
        name="attn_sample",
    )(q3, kn3, vn3, ckt3, cv3, lq1, lk1, lq2, lk2, sg)


def _merge_kernel(c_ref, o_ref, gc_ref, ga_ref, wco_ref, wao_ref, wout_ref, x_ref, y_ref):
    j = pl.program_id(1)

    @pl.when(j == 0)
    def _():
        y_ref[...] = x_ref[...]

    co = jnp.dot(c_ref[...], wco_ref[...], preferred_element_type=F32)
    ao = jnp.dot(o_ref[...], wao_ref[...], preferred_element_type=F32)
    merged = gc_ref[...].astype(F32) * co + ga_ref[...].astype(F32) * ao
    y_ref[...] += jnp.dot(merged.astype(BF16), wout_ref[...], preferred_element_type=F32)


def _merge(c2d, o2d, gates, wco, wao, wout, x2d, *, tm):
    n, d = x2d.shape
    dh = d // 2
    assert n % tm == 0
    return pl.pallas_call(
        _merge_kernel,
        grid=(n // tm, 2),
        in_specs=[
            pl.BlockSpec((tm, dh), lambda i, j: (i, 0)),
            pl.BlockSpec((tm, dh), lambda i, j: (i, 0)),
            pl.BlockSpec((tm, dh), lambda i, j: (i, j)),
            pl.BlockSpec((tm, dh), lambda i, j: (i, 2 + j)),
            pl.BlockSpec((dh, dh), lambda i, j: (0, j)),
            pl.BlockSpec((dh, dh), lambda i, j: (0, j)),
            pl.BlockSpec((dh, d), lambda i, j: (j, 0)),
            pl.BlockSpec((tm, d), lambda i, j: (i, 0)),
        ],
        out_specs=pl.BlockSpec((tm, d), lambda i, j: (i, 0)),
        out_shape=jax.ShapeDtypeStruct((n, d), F32),
        compiler_params=_cparams(("parallel", "arbitrary")),
        name="merge_out_proj",
    )(c2d, o2d, gates, gates, wco, wao, wout, x2d)


def _topk_rows(scores, cur_ref, rank_ref, val_ref, k):
    rows = scores.shape[0]

    cur_ref[...] = scores
    rank_ref[...] = jnp.full(rank_ref.shape, float(k), F32)

    def fast(j, m):
        cur = cur_ref[...]
        hit = cur == m
        rank_ref[...] = jnp.where(hit, lax.convert_element_type(j, F32), rank_ref[...])
        nxt = jnp.where(hit, -jnp.inf, cur)
        cur_ref[...] = nxt
        val_ref[j] = m
        return jnp.max(nxt, axis=0, keepdims=True)

    lax.fori_loop(0, k, fast, jnp.max(scores, axis=0, keepdims=True))
    ranked = jnp.sum(jnp.where(rank_ref[...] < float(k), 1.0, 0.0), axis=0, keepdims=True)
    tied = jnp.max(ranked) > float(k)

    @pl.when(tied)
    def _():
        iota = lax.broadcasted_iota(I32, (rows, scores.shape[1]), 0).astype(F32)
        cur_ref[...] = scores
        rank_ref[...] = jnp.full(rank_ref.shape, float(k), F32)

        def exact(j, carry):
            cur = cur_ref[...]
            m = jnp.max(cur, axis=0, keepdims=True)
            idx = jnp.min(jnp.where(cur == m, iota, float(rows)), axis=0, keepdims=True)
            hit = iota == idx
            rank_ref[...] = jnp.where(hit, lax.convert_element_type(j, F32), rank_ref[...])
            cur_ref[...] = jnp.where(hit, -jnp.inf, cur)
            val_ref[j] = m
            return carry

        lax.fori_loop(0, k, exact, 0)


def _cand_layout(kk):
    counts = [kk // (j1 + 1) for j1 in range(kk)]
    starts = [sum(counts[:j1]) for j1 in range(kk)]
    return counts, starts, sum(counts)


def _peer_route(x_ref, g2_ref, wq_ref, k1_ref, k2_ref, yt_scr, ht_scr, q_scr, e2_scr, rank2_scr, c_scr, n_scr,
                cur_scr, rank_scr, rank1_scr, cand_scr, crank_scr, val1_scr, val2_scr, valc_scr, *, tm):
    nk = k1_ref.shape[0]
    kk = PEER_TOPK
    counts, starts, ncand = _cand_layout(kk)
    x = x_ref[...]
    yt_scr[...] = jnp.zeros(yt_scr.shape, F32)
    ms = jnp.mean(x * x, axis=-1, keepdims=True)
    hf = x * lax.rsqrt(ms + EPS) * g2_ref[...]
    ht_scr[...] = hf.T.astype(BF16)
    q = jnp.dot(hf.astype(BF16), wq_ref[...], preferred_element_type=F32)
    for hd in range(PEER_HEADS):
        q_scr[hd] = q[:, hd * LANES:(hd + 1) * LANES]

    def head_body(hd, carry):
        qh = q_scr[hd]
        s1 = lax.dot_general(k1_ref[...], qh, NT_DIMS, preferred_element_type=F32)
        s2 = lax.dot_general(k2_ref[...], qh, NT_DIMS, preferred_element_type=F32)
        _topk_rows(s1, cur_scr, rank1_scr, val1_scr, kk)
        _topk_rows(s2, cur_scr, rank_scr, val2_scr, kk)
        rank2_scr[hd] = rank_scr[...].astype(BF16)
        v1max = val1_scr[0]
        v2max = val2_scr[0]
        v2all = jnp.concatenate([val2_scr[j2] for j2 in range(kk)], axis=0)
        pieces = [val1_scr[j1] + v2all[:counts[j1], :] for j1 in range(kk)]
        pieces.append(jnp.full((cand_scr.shape[0] - ncand, tm), -jnp.inf, F32))
        cand = jnp.concatenate(pieces, axis=0)
        _topk_rows(cand, cand_scr, crank_scr, valc_scr, kk)
        picked = crank_scr[...] < float(kk)
        z = jnp.sum(jnp.where(picked, jnp.exp(cand - valc_scr[0]), 0.0), axis=0, keepdims=True)
        crow = lax.broadcasted_iota(I32, cand.shape, 0)
        rank1 = rank1_scr[...]
        n_by = jnp.zeros((nk, tm), F32)
        for j1 in range(kk):
            in_row = (crow >= starts[j1]) & (crow < starts[j1] + counts[j1])
            n_j1 = jnp.sum(jnp.where(picked & in_row, 1.0, 0.0), axis=0, keepdims=True)
            n_by = n_by + jnp.where(rank1 == float(j1), n_j1, 0.0)
        n_scr[hd] = n_by
        c_scr[hd] = jnp.exp(s1 - v1max) / z
        e2_scr[hd] = jnp.exp(s2 - v2max).astype(BF16)
        return carry

    lax.fori_loop(0, PEER_HEADS, head_body, 0)


def _peer_scores(u_ref, ht_scr, at_scr):
    at_scr[...] = jnp.dot(u_ref[...], ht_scr[...], preferred_element_type=F32)


PEER_KEYS_PER_DOT = 2


def _peer_gates(chunk, at_scr, wt_scr, e2_scr, rank2_scr, c_scr, n_scr, *, tm, ce, nk):
    for blk in range(ce // nk):
        r = chunk * (ce // nk) + blk
        acc = jnp.zeros((nk, tm), BF16)
        for hd in range(PEER_HEADS):
            n_row = n_scr[hd, pl.ds(r, 1), :].astype(BF16)
            c_row = c_scr[hd, pl.ds(r, 1), :].astype(BF16)
            acc = acc + jnp.where(rank2_scr[hd] < n_row, e2_scr[hd] * c_row, 0)
        a = at_scr[blk * nk:(blk + 1) * nk, :]
        gelu = 0.5 * a * (1.0 + lax.erf(a * (2.0 ** -0.5)))
        wt_scr[blk * nk:(blk + 1) * nk, :] = acc * gelu.astype(BF16)


def _peer_combine(wt_scr, vt_ref, yt_scr, *, nk):
    per = PEER_KEYS_PER_DOT * nk
    total = None
    for g in range(wt_scr.shape[0] // per):
        rows = slice(g * per, (g + 1) * per)
        d = jnp.dot(vt_ref[:, rows], wt_scr[rows, :], preferred_element_type=F32)
        total = d if total is None else total + d
    yt_scr[...] += total


def _peer_kernel(x_ref, g2_ref, wq_ref, k1_ref, k2_ref, u_ref, vt_ref, y_ref,
                 ht_scr, yt_scr, q_scr, at0_scr, at1_scr, wt_scr, e2_scr, rank2_scr, c_scr, n_scr,
                 cur_scr, rank_scr, rank1_scr, cand_scr, crank_scr, val1_scr, val2_scr, valc_scr,
                 *, tm, ce):
    s = pl.program_id(1)
    last = pl.num_programs(1) - 1
    nk = k1_ref.shape[0]
    even = s % 2 == 0

    def step(score_into, mix_from):
        if mix_from is not None:
            _peer_gates(s - 1, mix_from, wt_scr, e2_scr, rank2_scr, c_scr, n_scr, tm=tm, ce=ce, nk=nk)
        if score_into is not None:
            _peer_scores(u_ref, ht_scr, score_into)
        if mix_from is not None:
            _peer_combine(wt_scr, vt_ref, yt_scr, nk=nk)

    @pl.when(s == 0)
    def _():
        _peer_route(x_ref, g2_ref, wq_ref, k1_ref, k2_ref, yt_scr, ht_scr, q_scr, e2_scr, rank2_scr, c_scr, n_scr,
                    cur_scr, rank_scr, rank1_scr, cand_scr, crank_scr, val1_scr, val2_scr, valc_scr, tm=tm)
        step(at0_scr, None)

    @pl.when((s > 0) & (s < last) & even)
    def _():
        step(at0_scr, at1_scr)

    @pl.when((s < last) & jnp.logical_not(even))
    def _():
        step(at1_scr, at0_scr)

    @pl.when(s == last)
    def _():
        step(None, at1_scr)
        y_ref[...] = x_ref[...] + yt_scr[...].T


def _peer(x2d, g2, wq, k1p, k2p, u_bf, vt_bf, *, tm, ce):
    n, d = x2d.shape
    ne = u_bf.shape[0]
    nk = k1p.shape[0]
    kk = PEER_TOPK
    nc = ne // ce
    ncand = -(-_cand_layout(kk)[2] // 8) * 8
    assert n % tm == 0 and ne % ce == 0 and ce % (2 * nk) == 0 and ne == nk * nk and nk == LANES
    assert nc % 2 == 0
    return pl.pallas_call(
        functools.partial(_peer_kernel, tm=tm, ce=ce),
        grid=(n // tm, nc + 1),
        in_specs=[
            pl.BlockSpec((tm, d), lambda i, s: (i, 0)),
            pl.BlockSpec((1, d), lambda i, s: (0, 0)),
            pl.BlockSpec(wq.shape, lambda i, s: (0, 0)),
            pl.BlockSpec((nk, LANES), lambda i, s: (0, 0)),
            pl.BlockSpec((nk, LANES), lambda i, s: (0, 0)),
            pl.BlockSpec((ce, d), lambda i, s: (jnp.minimum(s, nc - 1), 0)),
            pl.BlockSpec((d, ce), lambda i, s: (0, jnp.maximum(s - 1, 0))),
        ],
        out_specs=pl.BlockSpec((tm, d), lambda i, s: (i, 0)),
        out_shape=jax.ShapeDtypeStruct((n, d), F32),
        scratch_shapes=[
            pltpu.VMEM((d, tm), BF16),
            pltpu.VMEM((d, tm), F32),
            pltpu.VMEM((PEER_HEADS, tm, LANES), F32),
            pltpu.VMEM((ce, tm), F32),
            pltpu.VMEM((ce, tm), F32),
            pltpu.VMEM((ce, tm), BF16),
            pltpu.VMEM((PEER_HEADS, nk, tm), BF16),
            pltpu.VMEM((PEER_HEADS, nk, tm), BF16),
            pltpu.VMEM((PEER_HEADS, nk, tm), F32),
            pltpu.VMEM((PEER_HEADS, nk, tm), F32),
            pltpu.VMEM((nk, tm), F32),
            pltpu.VMEM((nk, tm), F32),
            pltpu.VMEM((nk, tm), F32),
            pltpu.VMEM((ncand, tm), F32),
            pltpu.VMEM((ncand, tm), F32),
            pltpu.VMEM((kk, 1, tm), F32),
            pltpu.VMEM((kk, 1, tm), F32),
            pltpu.VMEM((kk, 1, tm), F32),
        ],
        compiler_params=_cparams(("parallel", "arbitrary")),
        name="peer",
    )(x2d, g2, wq, k1p, k2p, u_bf, vt_bf)


def _tile128(g):
    return jnp.tile(g.reshape(1, -1), (1, LANES // g.shape[-1]))


def _layer(x3, pos_offset, hist, cache_k, cache_v, p, lam_init, *, tm, tt, tq, tm_peer, ce):
    bsz, t, d = x3.shape
    n = bsz * t
    x2d = x3.reshape(n, d)
    prompt = cache_k is None
    glu, q, k, kb, v, vb, gates = _in_proj(x2d, p["norm1_g"], p["w_in"], p["qg"], p["kg"],
                                           seq_len=t, pos_offset=pos_offset, tm=tm, k_transposed=prompt)
    dc = glu.shape[1]
    c, conv_state = _conv(glu.reshape(bsz, t, dc), hist, p["conv_dw_w"], p["conv_dw_b"],
                          p["conv_ln_g"], p["conv_ln_b"], tt=tt)
    lam_args = (p["lambda_q1"], p["lambda_k1"], p["lambda_q2"], p["lambda_k2"], p["subln_g"])
    r3 = lambda a: a.reshape(bsz, t, -1)
    if prompt:
        o = _attn_prompt(r3(q), kb, r3(vb), *lam_args, tq=tq, lam_init=lam_init)
        k = jnp.transpose(k.reshape(bsz, N_HEADS, 2, D_HEAD, t), (0, 4, 1, 2, 3))
    else:
        past = cache_k.shape[1]
        ckt = jnp.transpose(cache_k, (0, 2, 3, 4, 1)).reshape(bsz, -1, past)
        o = _attn_sample(r3(q), r3(kb), r3(vb), ckt, cache_v.reshape(bsz, past * N_HEADS, D_VHEAD), *lam_args,
                         lam_init=lam_init)
        k = k.reshape(bsz, t, N_HEADS, 2, D_HEAD)
    x_mid = _merge(c.reshape(n, dc), o.reshape(n, -1), gates, p["w_conv_out"], p["w_attn_out"],
                   p["w_out"], x2d, tm=tm)
    y = _peer(x_mid, p["norm2_g"], p["peer_wq"], p["k1p"], p["k2p"], p["peer_u"], p["peer_vt"],
              tm=tm_peer, ce=ce)
    return (y.reshape(bsz, t, d), k, v.reshape(bsz, t, N_HEADS, D_VHEAD), conv_state)


def kernel(x_prompt, x_sample, cache_attn_k, cache_attn_v, state_conv, norm1_g, w_in, conv_dw_w, conv_dw_b,
           conv_ln_g, conv_ln_b, w_conv_out, q_norm_g, k_norm_g, lambda_q1, lambda_k1, lambda_q2, lambda_k2,
           subln_g, w_attn_out, w_out, norm2_g, peer_wq, peer_k1, peer_k2, peer_u, peer_v):
    depth = w_in.shape[0]
    xp, xs = x_prompt, x_sample
    outs = [[] for _ in range(6)]
    row = lambda a: a.reshape(1, -1)
    for l in range(depth):
        half = peer_k1.shape[-1]
        p = dict(
            norm1_g=row(norm1_g[l]), w_in=w_in[l].astype(BF16),
            conv_dw_w=conv_dw_w[l], conv_dw_b=row(conv_dw_b[l]),
            conv_ln_g=row(conv_ln_g[l]), conv_ln_b=row(conv_ln_b[l]),
            w_conv_out=w_conv_out[l].astype(BF16),
            qg=_tile128(q_norm_g[l]), kg=_tile128(k_norm_g[l]),
            lambda_q1=row(lambda_q1[l]), lambda_k1=row(lambda_k1[l]),
            lambda_q2=row(lambda_q2[l]), lambda_k2=row(lambda_k2[l]),
            subln_g=row(subln_g[l]),
            w_attn_out=w_attn_out[l].astype(BF16), w_out=w_out[l].astype(BF16),
            norm2_g=row(norm2_g[l]), peer_wq=peer_wq[l].astype(BF16),
            k1p=jnp.pad(peer_k1[l], ((0, 0), (0, LANES - half))),
            k2p=jnp.pad(peer_k2[l], ((0, 0), (LANES - half, 0))),
            peer_u=peer_u[l].astype(BF16), peer_vt=peer_v[l].astype(BF16).T,
        )
        li = _lambda_init(l)
        bp, tp, _ = xp.shape
        bs, ts, _ = xs.shape
        zero_hist = jnp.zeros((bp, CONV_PAD, state_conv.shape[-1]), xp.dtype)
        xp, kp, vp, cp = _layer(xp, 0, zero_hist, None, None, p, li,
                                tm=min(512, bp * tp), tt=min(256, tp), tq=min(512, tp),
                                tm_peer=min(512, bp * tp), ce=512)
        hist_s = jnp.pad(state_conv[l], ((0, 0), (CONV_PAD - CONV_STATE, 0), (0, 0)))
        xs, ks, vs, cs = _layer(xs, cache_attn_k.shape[2], hist_s, cache_attn_k[l], cache_attn_v[l], p, li,
                                tm=min(256, bs * ts), tt=ts, tq=ts,
                                tm_peer=min(256, bs * ts), ce=512)
        for lst, val in zip(outs, (kp, vp, cp, ks, vs, cs)):
            lst.append(val)
    kp, vp, cp, ks, vs, cs = (jnp.stack(o) for o in outs)
    return (xp, xs, kp, vp, cp, ks, vs, cs)
```

```python
import functools
import math

import jax
import jax.numpy as jnp
from jax import lax
from jax.experimental import pallas as pl
from jax.experimental.pallas import tpu as pltpu

F32 = jnp.float32
BF16 = jnp.bfloat16
I32 = jnp.int32

LANES = 128
SUBLANES = 8
CHUNK = 64
EPS = 1e-6
N_HEADS = 8
D_HEAD = 64
D_VHEAD = 2 * D_HEAD
CONV_WIDTH = 31
CONV_STATE = CONV_WIDTH - 1
CONV_PAD = 32
ROPE_THETA = 10000.0
NEG_INF = -1e30
PEER_HEADS = 8
PEER_TOPK = 16
PEER_CHUNK = 512
CAST_ROWS = 512
VMEM_LIMIT = 56 * 1024 * 1024

NT_DIMS = (((1,), (1,)), ((), ()))
TN_DIMS = (((0,), (0,)), ((), ()))


def _cparams(sem):
    return pltpu.CompilerParams(dimension_semantics=sem, vmem_limit_bytes=VMEM_LIMIT)


def _lambda_init(layer):
    return 0.8 - 0.6 * math.exp(-0.3 * layer)


def _cast_kernel(w_ref, o_ref, *, transpose):
    w = w_ref[...]
    o_ref[0] = (w.T if transpose else w).astype(BF16)


def _bf16_col_blocks(w, width, rows):
    k, n = w.shape
    assert k % rows == 0 and n % width == 0
    return pl.pallas_call(
        functools.partial(_cast_kernel, transpose=False),
        grid=(n // width, k // rows),
        in_specs=[pl.BlockSpec((rows, width), lambda j, r: (r, j))],
        out_specs=pl.BlockSpec((1, rows, width), lambda j, r: (j, r, 0)),
        out_shape=jax.ShapeDtypeStruct((n // width, k, width), BF16),
        compiler_params=_cparams(("parallel", "parallel")),
        name="cast_col_blocks",
    )(w)


def _bf16_row_blocks_transposed(w, rows):
    e, d = w.shape
    assert e % rows == 0
    return pl.pallas_call(
        functools.partial(_cast_kernel, transpose=True),
        grid=(e // rows,),
        in_specs=[pl.BlockSpec((rows, d), lambda c: (c, 0))],
        out_specs=pl.BlockSpec((1, d, rows), lambda c: (c, 0, 0)),
        out_shape=jax.ShapeDtypeStruct((e // rows, d, rows), BF16),
        compiler_params=_cparams(("parallel",)),
        name="cast_row_blocks_t",
    )(w)


def _qk_norm_rope(z, g, cos, sin_signed):
    tm = z.shape[0]
    lane = lax.broadcasted_iota(I32, (tm, LANES), 1)
    lo = lane < D_HEAD
    first = (lane & (D_HEAD // 2)) == 0
    outs = []
    for c in range(z.shape[1] // LANES):
        zc = z[:, c * LANES:(c + 1) * LANES]
        zz = zc * zc
        s_lo = jnp.sum(jnp.where(lo, zz, 0.0), axis=-1, keepdims=True)
        s_hi = jnp.sum(jnp.where(lo, 0.0, zz), axis=-1, keepdims=True)
        r = jnp.where(lo, lax.rsqrt(s_lo * (1.0 / D_HEAD) + EPS), lax.rsqrt(s_hi * (1.0 / D_HEAD) + EPS))
        y = zc * r * g
        up = pltpu.roll(y, LANES - D_HEAD // 2, 1)
        dn = pltpu.roll(y, D_HEAD // 2, 1)
        outs.append(y * cos + jnp.where(first, up, dn) * sin_signed)
    return jnp.concatenate(outs, axis=1)


def _in_proj_kernel(x_ref, g1_ref, w_ref, qg_ref, kg_ref,
                    glu_ref, q_ref, k_ref, kb_ref, v_ref, vb_ref, gate_ref,
                    h_scr, za_scr, z0_scr, z1_scr, cos_scr, sin_scr,
                    *, seq_len, pos_offset, tm, k_transposed, n_blocks):
    i = pl.program_id(0)
    j = pl.program_id(1)
    z_scr = (z0_scr, z1_scr)

    def prologue():
        x = x_ref[...]
        ms = jnp.mean(x * x, axis=-1, keepdims=True)
        h_scr[...] = (x * lax.rsqrt(ms + EPS) * g1_ref[...]).astype(BF16)
        row = lax.broadcasted_iota(I32, (tm, LANES), 0) + i * tm
        pos = (row & (seq_len - 1)) + pos_offset
        lane = lax.broadcasted_iota(I32, (tm, LANES), 1)
        f = (lane & (D_HEAD // 2 - 1)).astype(F32)
        inv = jnp.power(jnp.float32(ROPE_THETA), -f / (D_HEAD // 2))
        ang = pos.astype(F32) * inv
        cos_scr[...] = jnp.cos(ang)
        s = jnp.sin(ang)
        sin_scr[...] = jnp.where((lane & (D_HEAD // 2)) == 0, -s, s)

    def finish(blk, z):
        if blk == 0:
            za_scr[...] = z
        elif blk == 1:
            glu_ref[...] = za_scr[...] * jax.nn.sigmoid(z)
        elif blk == 2:
            q = _qk_norm_rope(z, qg_ref[...], cos_scr[...], sin_scr[...])
            q_ref[...] = (q * (D_HEAD ** -0.5)).astype(BF16)
        elif blk == 3:
            k = _qk_norm_rope(z, kg_ref[...], cos_scr[...], sin_scr[...])
            if k_transposed:
                kt = k.T
                k_ref[0] = kt
                kb_ref[0] = kt.astype(BF16)
            else:
                k_ref[...] = k
                kb_ref[...] = k.astype(BF16)
        elif blk == 4:
            v_ref[...] = z
            vb_ref[...] = z.astype(BF16)
        else:
            gate_ref[...] = jax.nn.sigmoid(z).astype(BF16)

    for step in range(n_blocks + 1):
        @pl.when(j == step)
        def _(step=step):
            if step == 0:
                prologue()
            if step >= 1:
                finish(step - 1, z_scr[(step - 1) % 2][...])
            if step < n_blocks:
                z_scr[step % 2][...] = jnp.dot(h_scr[...], w_ref[0], preferred_element_type=F32)


def _in_proj(x2d, g1, w_in_blocks, qg, kg, *, seq_len, pos_offset, tm, k_transposed):
    n, d = x2d.shape
    nj, _, tn = w_in_blocks.shape
    assert n % tm == 0 and tn == d // 2 and nj == 9
    assert seq_len & (seq_len - 1) == 0
    blk = lambda: pl.BlockSpec((tm, tn), lambda i, j: (i, 0))
    if k_transposed:
        assert seq_len % tm == 0
        per = seq_len // tm
        k_shape = (n // seq_len, tn, seq_len)
        kblk = lambda: pl.BlockSpec((1, tn, tm), lambda i, j: (i // per, 0, i % per))
    else:
        k_shape = (n, tn)
        kblk = blk
    out_shape = (
        jax.ShapeDtypeStruct((n, tn), F32),
        jax.ShapeDtypeStruct((n, tn), BF16),
        jax.ShapeDtypeStruct(k_shape, F32),
        jax.ShapeDtypeStruct(k_shape, BF16),
        jax.ShapeDtypeStruct((n, tn), F32),
        jax.ShapeDtypeStruct((n, tn), BF16),
        jax.ShapeDtypeStruct((n, 4 * tn), BF16),
    )
    return pl.pallas_call(
        functools.partial(_in_proj_kernel, seq_len=seq_len, pos_offset=pos_offset, tm=tm,
                          k_transposed=k_transposed, n_blocks=nj),
        grid=(n // tm, nj + 1),
        in_specs=[
            pl.BlockSpec((tm, d), lambda i, j: (i, 0)),
            pl.BlockSpec((1, d), lambda i, j: (0, 0)),
            pl.BlockSpec((1, d, tn), lambda i, j: (jnp.minimum(j, nj - 1), 0, 0)),
            pl.BlockSpec((1, LANES), lambda i, j: (0, 0)),
            pl.BlockSpec((1, LANES), lambda i, j: (0, 0)),
        ],
        out_specs=(blk(), blk(), kblk(), kblk(), blk(), blk(),
                   pl.BlockSpec((tm, tn), lambda i, j: (i, jnp.maximum(j - 6, 0)))),
        out_shape=out_shape,
        scratch_shapes=[
            pltpu.VMEM((tm, d), BF16),
            pltpu.VMEM((tm, tn), F32),
            pltpu.VMEM((tm, tn), F32),
            pltpu.VMEM((tm, tn), F32),
            pltpu.VMEM((tm, LANES), F32),
            pltpu.VMEM((tm, LANES), F32),
        ],
        compiler_params=_cparams(("parallel", "arbitrary")),
        name="in_proj",
    )(x2d, g1, w_in_blocks, qg, kg)


CONV_ROWS = 64


def _conv_kernel(cur_ref, prev_ref, st_ref, w_ref, b_ref, lg_ref, lb_ref,
                 c_ref, so_ref, buf, shifted, *, tt):
    ti = pl.program_id(1)
    dc = cur_ref.shape[2]

    @pl.when(ti == 0)
    def _():
        buf[0:CONV_PAD, :] = st_ref[0]

    @pl.when(ti > 0)
    def _():
        buf[0:CONV_PAD, :] = prev_ref[0]

    buf[CONV_PAD:CONV_PAD + tt, :] = cur_ref[0]
    lead = CONV_PAD - CONV_STATE
    rows = min(CONV_ROWS, tt)
    span = shifted.shape[1]
    for p in range(1, SUBLANES):
        shifted[p - 1] = buf[p:p + span, :]

    for rs in range(tt // rows):
        parts = []
        for cs in range(dc // LANES):
            sl = slice(cs * LANES, (cs + 1) * LANES)
            acc = jnp.zeros((rows, LANES), F32)
            for j in range(CONV_WIDTH):
                phase = (lead + j) % SUBLANES
                r0 = lead + j - phase + rs * rows
                src = buf if phase == 0 else shifted.at[phase - 1]
                acc = acc + src[r0:r0 + rows, sl] * w_ref[j:j + 1, sl]
            parts.append(acc + b_ref[:, sl])
        c = jnp.concatenate(parts, axis=1)
        mu = jnp.mean(c, axis=-1, keepdims=True)
        var = jnp.mean(jnp.square(c - mu), axis=-1, keepdims=True)
        y = (c - mu) * lax.rsqrt(var + EPS) * lg_ref[...] + lb_ref[...]
        c_ref[0, rs * rows:(rs + 1) * rows, :] = (y * jax.nn.sigmoid(y)).astype(BF16)

    @pl.when(ti == pl.num_programs(1) - 1)
    def _():
        so_ref[0] = buf[lead + tt:lead + tt + CONV_STATE, :]


def _conv(glu3d, hist, w, b, lg, lb, *, tt):
    bsz, t, dc = glu3d.shape
    assert t % tt == 0 and tt % min(CONV_ROWS, tt) == 0 and tt % CONV_PAD == 0
    per = tt // CONV_PAD
    return pl.pallas_call(
        functools.partial(_conv_kernel, tt=tt),
        grid=(bsz, t // tt),
        in_specs=[
            pl.BlockSpec((1, tt, dc), lambda bi, ti: (bi, ti, 0)),
            pl.BlockSpec((1, CONV_PAD, dc), lambda bi, ti: (bi, jnp.maximum(ti * per - 1, 0), 0)),
            pl.BlockSpec((1, CONV_PAD, dc), lambda bi, ti: (bi, 0, 0)),
            pl.BlockSpec((CONV_WIDTH, dc), lambda bi, ti: (0, 0)),
            pl.BlockSpec((1, dc), lambda bi, ti: (0, 0)),
            pl.BlockSpec((1, dc), lambda bi, ti: (0, 0)),
            pl.BlockSpec((1, dc), lambda bi, ti: (0, 0)),
        ],
        out_specs=(
            pl.BlockSpec((1, tt, dc), lambda bi, ti: (bi, ti, 0)),
            pl.BlockSpec((1, CONV_STATE, dc), lambda bi, ti: (bi, 0, 0)),
        ),
        out_shape=(
            jax.ShapeDtypeStruct((bsz, t, dc), BF16),
            jax.ShapeDtypeStruct((bsz, CONV_STATE, dc), F32),
        ),
        scratch_shapes=[
            pltpu.VMEM((CONV_PAD + tt, dc), F32),
            pltpu.VMEM((SUBLANES - 1, tt + CONV_PAD - SUBLANES, dc), F32),
        ],
        compiler_params=_cparams(("parallel", "arbitrary")),
        name="conv_branch",
    )(glu3d, glu3d, hist, w, b, lg, lb)


def _lambda_value(lq1_ref, lk1_ref, lq2_ref, lk2_ref, lam_init):
    a = jnp.sum(lq1_ref[...] * lk1_ref[...], axis=-1, keepdims=True)
    b = jnp.sum(lq2_ref[...] * lk2_ref[...], axis=-1, keepdims=True)
    return jnp.exp(a) - jnp.exp(b) + lam_init


def _head_finish(o1, o2, lam, sg, lam_init):
    o = o1 - lam * o2
    ms = jnp.mean(o * o, axis=-1, keepdims=True)
    return (o * lax.rsqrt(ms + EPS) * sg) * (1.0 - lam_init)


def _split_heads(q):
    lane = lax.broadcasted_iota(I32, q.shape, 1)
    zero = jnp.zeros_like(q)
    return jnp.where(lane < D_HEAD, q, zero), jnp.where(lane >= D_HEAD, q, zero)


ATTN_FULL_FULL, ATTN_FULL_DIAG, ATTN_DIAG = 0, 1, 2


def _attn_blocks(qcs, blocks, state):
    m, acc = state
    scores = []
    for kt, _, diagonal in blocks:
        ss = [jnp.dot(qc, kt, preferred_element_type=F32) for qc in qcs]
        if diagonal:
            tq, tk = ss[0].shape
            row_chunk = lax.broadcasted_iota(I32, (tq, tk), 0) // CHUNK
            col_chunk = lax.broadcasted_iota(I32, (tq, tk), 1) // CHUNK
            mask = col_chunk <= row_chunk
            ss = [jnp.where(mask, s, NEG_INF) for s in ss]
        scores.append(ss)
    for ss, (_, v, _) in zip(scores, blocks):
        tk = ss[0].shape[1]
        v_ext = jnp.concatenate([v, jnp.ones_like(v)], axis=1)
        m_new = [jnp.maximum(m[c], jnp.max(ss[c], axis=-1, keepdims=True)) for c in range(2)]
        ps = [jnp.exp(ss[c] - jnp.concatenate([m_new[c]] * (tk // LANES), axis=1)).astype(BF16) for c in range(2)]
        pvs = [jnp.dot(ps[c], v_ext, preferred_element_type=F32) for c in range(2)]
        alphas = [jnp.exp(m[c] - m_new[c]) for c in range(2)]
        acc = [jnp.concatenate([alphas[c], alphas[c]], axis=1) * acc[c] + pvs[c] for c in range(2)]
        m = m_new
    return m, acc


def _attn_prompt_kernel(qi_tab, ka_tab, kb_tab, kind_tab, q_ref, kta_ref, ktb_ref, va_ref, vb_ref,
                        lq1_ref, lk1_ref, lq2_ref, lk2_ref, sg_ref, o_ref, m_scr, acc_scr, *, lam_init):
    s_id = pl.program_id(2)
    kind = kind_tab[s_id]

    @pl.when(ka_tab[s_id] == 0)
    def _():
        m_scr[...] = jnp.full(m_scr.shape, -jnp.inf, F32)
        acc_scr[...] = jnp.zeros(acc_scr.shape, F32)

    def load():
        return _split_heads(q_ref[0]), ([m_scr[c] for c in range(2)], [acc_scr[c] for c in range(2)])

    def store(state):
        for c in range(2):
            m_scr[c] = state[0][c]
            acc_scr[c] = state[1][c]

    def finish(state):
        lam = _lambda_value(lq1_ref, lk1_ref, lq2_ref, lk2_ref, lam_init)
        a1, a2 = state[1]
        y = _head_finish(a1[:, :D_VHEAD] / a1[:, D_VHEAD:], a2[:, :D_VHEAD] / a2[:, D_VHEAD:],
                         lam, sg_ref[...], lam_init)
        o_ref[0] = y.astype(BF16)

    @pl.when(kind == ATTN_FULL_FULL)
    def _():
        qcs, state = load()
        store(_attn_blocks(qcs, [(kta_ref[0], va_ref[0], False), (ktb_ref[0], vb_ref[0], False)], state))

    @pl.when(kind == ATTN_FULL_DIAG)
    def _():
        qcs, state = load()
        finish(_attn_blocks(qcs, [(kta_ref[0], va_ref[0], False), (ktb_ref[0], vb_ref[0], True)], state))

    @pl.when(kind == ATTN_DIAG)
    def _():
        qcs, state = load()
        finish(_attn_blocks(qcs, [(kta_ref[0], va_ref[0], True)], state))


def _attn_steps(nq):
    steps = []
    for qi in range(nq):
        ki = 0
        while ki + 1 < qi:
            steps.append((qi, ki, ki + 1, ATTN_FULL_FULL))
            ki += 2
        if ki + 1 == qi:
            steps.append((qi, ki, qi, ATTN_FULL_DIAG))
        else:
            steps.append((qi, qi, qi, ATTN_DIAG))
    return steps


def _attn_prompt(q3, kt3, v3, lq1, lk1, lq2, lk2, sg, *, tq, lam_init):
    bsz, t, da = q3.shape
    nq = t // tq
    assert t % tq == 0 and tq % CHUNK == 0 and tq % LANES == 0
    steps = _attn_steps(nq)
    tabs = [jnp.asarray([st[i] for st in steps], I32) for i in range(4)]
    vec = lambda n: pl.BlockSpec((1, n), lambda b, h, s, *_: (0, 0))
    grid_spec = pltpu.PrefetchScalarGridSpec(
        num_scalar_prefetch=4,
        grid=(bsz, N_HEADS, len(steps)),
        in_specs=[
            pl.BlockSpec((1, tq, LANES), lambda b, h, s, qt, ka, kb, kd: (b, qt[s], h)),
            pl.BlockSpec((1, LANES, tq), lambda b, h, s, qt, ka, kb, kd: (b, h, ka[s])),
            pl.BlockSpec((1, LANES, tq), lambda b, h, s, qt, ka, kb, kd: (b, h, kb[s])),
            pl.BlockSpec((1, tq, LANES), lambda b, h, s, qt, ka, kb, kd: (b, ka[s], h)),
            pl.BlockSpec((1, tq, LANES), lambda b, h, s, qt, ka, kb, kd: (b, kb[s], h)),
            vec(D_HEAD), vec(D_HEAD), vec(D_HEAD), vec(D_HEAD), vec(D_VHEAD),
        ],
        out_specs=pl.BlockSpec((1, tq, LANES), lambda b, h, s, qt, ka, kb, kd: (b, qt[s], h)),
        scratch_shapes=[
            pltpu.VMEM((2, tq, LANES), F32),
            pltpu.VMEM((2, tq, 2 * D_VHEAD), F32),
        ],
    )
    return pl.pallas_call(
        functools.partial(_attn_prompt_kernel, lam_init=lam_init),
        grid_spec=grid_spec,
        out_shape=jax.ShapeDtypeStruct((bsz, t, da), BF16),
        compiler_params=_cparams(("parallel", "parallel", "arbitrary")),
        name="attn_prompt",
    )(*tabs, q3, kt3, kt3, v3, v3, lq1, lk1, lq2, lk2, sg)


def _attn_sample_kernel(q_ref, kn_ref, vn_ref, ck_ref, cv_ref, lq1_ref, lk1_ref, lq2_ref, lk2_ref, sg_ref,
                        o_ref, *, lam_init):
    past = ck_ref.shape[2]
    lam = _lambda_value(lq1_ref, lk1_ref, lq2_ref, lk2_ref, lam_init)
    for h in range(N_HEADS):
        sl = slice(h * LANES, (h + 1) * LANES)
        q = q_ref[0, :, sl]
        kn = kn_ref[0, :, sl]
        vn = vn_ref[0, :, sl]
        kpt = ck_ref[0, sl, :].astype(BF16)
        vp = cv_ref[0, pl.ds(h, past, stride=N_HEADS), :].astype(BF16)
        outs = []
        for qc in _split_heads(q):
            sp = jnp.dot(qc, kpt, preferred_element_type=F32)
            sn = lax.dot_general(qc, kn, NT_DIMS, preferred_element_type=F32)
            m = jnp.maximum(jnp.max(sp, axis=-1, keepdims=True), jnp.max(sn, axis=-1, keepdims=True))
            pp = jnp.exp(sp - m)
            pn = jnp.exp(sn - m)
            l = jnp.sum(pp, axis=-1, keepdims=True) + jnp.sum(pn, axis=-1, keepdims=True)
            o = (jnp.dot(pp.astype(BF16), vp, preferred_element_type=F32)
                 + jnp.dot(pn.astype(BF16), vn, preferred_element_type=F32))
            outs.append(o / l)
        o_ref[0, :, sl] = _head_finish(outs[0], outs[1], lam, sg_ref[...], lam_init).astype(BF16)


def _attn_sample(q3, kn3, vn3, ckt3, cv3, lq1, lk1, lq2, lk2, sg, *, lam_init):
    bsz, t, da = q3.shape
    past = ckt3.shape[2]
    vec = lambda n: pl.BlockSpec((1, n), lambda b: (0, 0))
    new = lambda: pl.BlockSpec((1, t, da), lambda b: (b, 0, 0))
    return pl.pallas_call(
        functools.partial(_attn_sample_kernel, lam_init=lam_init),
        grid=(bsz,),
        in_specs=[new(), new(), new(),
                  pl.BlockSpec((1, da, past), lambda b: (b, 0, 0)),
                  pl.BlockSpec((1, past * N_HEADS, D_VHEAD), lambda b: (b, 0, 0)),
                  vec(D_HEAD), vec(D_HEAD), vec(D_HEAD), vec(D_HEAD), vec(D_VHEAD)],
        out_specs=new(),
        out_shape=jax.ShapeDtypeStruct((bsz, t, da), BF16),
        compiler_params=_cparams(("parallel",)),
        name="attn_sample",
    )(q3, kn3, vn3, ckt3, cv3, lq1, lk1, lq2, lk2, sg)


def _merge_kernel(c_ref, o_ref, gc_ref, ga_ref, wco_ref, wao_ref, wout_ref, x_ref, y_ref):
    j = pl.program_id(1)

    @pl.when(j == 0)
    def _():
        y_ref[...] = x_ref[...]

    co = jnp.dot(c_ref[...], wco_ref[0], preferred_element_type=F32)
    ao = jnp.dot(o_ref[...], wao_ref[0], preferred_element_type=F32)
    merged = gc_ref[...].astype(F32) * co + ga_ref[...].astype(F32) * ao
    y_ref[...] += jnp.dot(merged.astype(BF16), wout_ref[...], preferred_element_type=F32)


def _merge(c2d, o2d, gates, wco, wao, wout, x2d, *, tm):
    n, d = x2d.shape
    dh = d // 2
    assert n % tm == 0
    return pl.pallas_call(
        _merge_kernel,
        grid=(n // tm, 2),
        in_specs=[
            pl.BlockSpec((tm, dh), lambda i, j: (i, 0)),
            pl.BlockSpec((tm, dh), lambda i, j: (i, 0)),
            pl.BlockSpec((tm, dh), lambda i, j: (i, j)),
            pl.BlockSpec((tm, dh), lambda i, j: (i, 2 + j)),
            pl.BlockSpec((1, dh, dh), lambda i, j: (j, 0, 0)),
            pl.BlockSpec((1, dh, dh), lambda i, j: (j, 0, 0)),
            pl.BlockSpec((dh, d), lambda i, j: (j, 0)),
            pl.BlockSpec((tm, d), lambda i, j: (i, 0)),
        ],
        out_specs=pl.BlockSpec((tm, d), lambda i, j: (i, 0)),
        out_shape=jax.ShapeDtypeStruct((n, d), F32),
        compiler_params=_cparams(("parallel", "arbitrary")),
        name="merge_out_proj",
    )(c2d, o2d, gates, gates, wco, wao, wout, x2d)


def _topk_rows(scores, cur_ref, rank_ref, val_ref, k):
    rows = scores.shape[0]

    cur_ref[...] = scores
    rank_ref[...] = jnp.full(rank_ref.shape, float(k), F32)

    def fast(j, m):
        cur = cur_ref[...]
        hit = cur == m
        rank_ref[...] = jnp.where(hit, lax.convert_element_type(j, F32), rank_ref[...])
        nxt = jnp.where(hit, -jnp.inf, cur)
        cur_ref[...] = nxt
        val_ref[j] = m
        return jnp.max(nxt, axis=0, keepdims=True)

    lax.fori_loop(0, k, fast, jnp.max(scores, axis=0, keepdims=True))
    ranked = jnp.sum(jnp.where(rank_ref[...] < float(k), 1.0, 0.0), axis=0, keepdims=True)
    tied = jnp.max(ranked) > float(k)

    @pl.when(tied)
    def _():
        iota = lax.broadcasted_iota(I32, (rows, scores.shape[1]), 0).astype(F32)
        cur_ref[...] = scores
        rank_ref[...] = jnp.full(rank_ref.shape, float(k), F32)

        def exact(j, carry):
            cur = cur_ref[...]
            m = jnp.max(cur, axis=0, keepdims=True)
            idx = jnp.min(jnp.where(cur == m, iota, float(rows)), axis=0, keepdims=True)
            hit = iota == idx
            rank_ref[...] = jnp.where(hit, lax.convert_element_type(j, F32), rank_ref[...])
            cur_ref[...] = jnp.where(hit, -jnp.inf, cur)
            val_ref[j] = m
            return carry

        lax.fori_loop(0, k, exact, 0)


def _cand_layout(kk):
    counts = [kk // (j1 + 1) for j1 in range(kk)]
    starts = [sum(counts[:j1]) for j1 in range(kk)]
    return counts, starts, sum(counts)


def _peer_route(x_ref, g2_ref, wq_ref, k1_ref, k2_ref, yt_scr, ht_scr, q_scr, e2_scr, rank2_scr, c_scr, n_scr,
                cur_scr, rank_scr, rank1_scr, cand_scr, crank_scr, val1_scr, val2_scr, valc_scr, *, tm):
    nk = k1_ref.shape[0]
    kk = PEER_TOPK
    counts, starts, ncand = _cand_layout(kk)
    x = x_ref[...]
    yt_scr[...] = jnp.zeros(yt_scr.shape, F32)
    ms = jnp.mean(x * x, axis=-1, keepdims=True)
    hf = x * lax.rsqrt(ms + EPS) * g2_ref[...]
    ht_scr[...] = hf.T.astype(BF16)
    q = jnp.dot(hf.astype(BF16), wq_ref[...], preferred_element_type=F32)
    for hd in range(PEER_HEADS):
        q_scr[hd] = q[:, hd * LANES:(hd + 1) * LANES]

    def head_body(hd, carry):
        qh = q_scr[hd]
        s1 = lax.dot_general(k1_ref[...], qh, NT_DIMS, preferred_element_type=F32)
        s2 = lax.dot_general(k2_ref[...], qh, NT_DIMS, preferred_element_type=F32)
        _topk_rows(s1, cur_scr, rank1_scr, val1_scr, kk)
        _topk_rows(s2, cur_scr, rank_scr, val2_scr, kk)
        rank2_scr[hd] = rank_scr[...].astype(BF16)
        v1max = val1_scr[0]
        v2max = val2_scr[0]
        v2all = jnp.concatenate([val2_scr[j2] for j2 in range(kk)], axis=0)
        pieces = [val1_scr[j1] + v2all[:counts[j1], :] for j1 in range(kk)]
        pieces.append(jnp.full((cand_scr.shape[0] - ncand, tm), -jnp.inf, F32))
        cand = jnp.concatenate(pieces, axis=0)
        _topk_rows(cand, cand_scr, crank_scr, valc_scr, kk)
        picked = crank_scr[...] < float(kk)
        z = jnp.sum(jnp.where(picked, jnp.exp(cand - valc_scr[0]), 0.0), axis=0, keepdims=True)
        crow = lax.broadcasted_iota(I32, cand.shape, 0)
        rank1 = rank1_scr[...]
        n_by = jnp.zeros((nk, tm), F32)
        for j1 in range(kk):
            in_row = (crow >= starts[j1]) & (crow < starts[j1] + counts[j1])
            n_j1 = jnp.sum(jnp.where(picked & in_row, 1.0, 0.0), axis=0, keepdims=True)
            n_by = n_by + jnp.where(rank1 == float(j1), n_j1, 0.0)
        n_scr[hd] = n_by
        c_scr[hd] = jnp.exp(s1 - v1max) / z
        e2_scr[hd] = jnp.exp(s2 - v2max).astype(BF16)
        return carry

    lax.fori_loop(0, PEER_HEADS, head_body, 0)


def _peer_scores(u_ref, ht_scr, at_scr):
    at_scr[...] = jnp.dot(u_ref[...], ht_scr[...], preferred_element_type=F32)


PEER_KEYS_PER_DOT = 2


def _peer_gates(chunk, at_scr, wt_scr, e2_scr, rank2_scr, c_scr, n_scr, *, tm, ce, nk):
    for blk in range(ce // nk):
        r = chunk * (ce // nk) + blk
        acc = jnp.zeros((nk, tm), BF16)
        for hd in range(PEER_HEADS):
            n_row = n_scr[hd, pl.ds(r, 1), :].astype(BF16)
            c_row = c_scr[hd, pl.ds(r, 1), :].astype(BF16)
            acc = acc + jnp.where(rank2_scr[hd] < n_row, e2_scr[hd] * c_row, 0)
        a = at_scr[blk * nk:(blk + 1) * nk, :]
        gelu = 0.5 * a * (1.0 + lax.erf(a * (2.0 ** -0.5)))
        wt_scr[blk * nk:(blk + 1) * nk, :] = acc * gelu.astype(BF16)


def _peer_combine(wt_scr, vt_ref, yt_scr, *, nk):
    per = PEER_KEYS_PER_DOT * nk
    total = None
    for g in range(wt_scr.shape[0] // per):
        rows = slice(g * per, (g + 1) * per)
        d = jnp.dot(vt_ref[0, :, rows], wt_scr[rows, :], preferred_element_type=F32)
        total = d if total is None else total + d
    yt_scr[...] += total


def _peer_kernel(x_ref, g2_ref, wq_ref, k1_ref, k2_ref, u_ref, vt_ref, y_ref,
                 ht_scr, yt_scr, q_scr, at0_scr, at1_scr, wt_scr, e2_scr, rank2_scr, c_scr, n_scr,
                 cur_scr, rank_scr, rank1_scr, cand_scr, crank_scr, val1_scr, val2_scr, valc_scr,
                 *, tm, ce):
    s = pl.program_id(1)
    last = pl.num_programs(1) - 1
    nk = k1_ref.shape[0]
    even = s % 2 == 0

    def step(score_into, mix_from):
        if mix_from is not None:
            _peer_gates(s - 1, mix_from, wt_scr, e2_scr, rank2_scr, c_scr, n_scr, tm=tm, ce=ce, nk=nk)
        if score_into is not None:
            _peer_scores(u_ref, ht_scr, score_into)
        if mix_from is not None:
            _peer_combine(wt_scr, vt_ref, yt_scr, nk=nk)

    @pl.when(s == 0)
    def _():
        _peer_route(x_ref, g2_ref, wq_ref, k1_ref, k2_ref, yt_scr, ht_scr, q_scr, e2_scr, rank2_scr, c_scr, n_scr,
                    cur_scr, rank_scr, rank1_scr, cand_scr, crank_scr, val1_scr, val2_scr, valc_scr, tm=tm)
        step(at0_scr, None)

    @pl.when((s > 0) & (s < last) & even)
    def _():
        step(at0_scr, at1_scr)

    @pl.when((s < last) & jnp.logical_not(even))
    def _():
        step(at1_scr, at0_scr)

    @pl.when(s == last)
    def _():
        step(None, at1_scr)
        y_ref[...] = x_ref[...] + yt_scr[...].T


def _peer(x2d, g2, wq, k1p, k2p, u_bf, vt_bf, *, tm, ce):
    n, d = x2d.shape
    ne = u_bf.shape[0]
    nk = k1p.shape[0]
    kk = PEER_TOPK
    nc = ne // ce
    ncand = -(-_cand_layout(kk)[2] // 8) * 8
    assert n % tm == 0 and ne % ce == 0 and ce % (2 * nk) == 0 and ne == nk * nk and nk == LANES
    assert nc % 2 == 0
    return pl.pallas_call(
        functools.partial(_peer_kernel, tm=tm, ce=ce),
        grid=(n // tm, nc + 1),
        in_specs=[
            pl.BlockSpec((tm, d), lambda i, s: (i, 0)),
            pl.BlockSpec((1, d), lambda i, s: (0, 0)),
            pl.BlockSpec(wq.shape, lambda i, s: (0, 0)),
            pl.BlockSpec((nk, LANES), lambda i, s: (0, 0)),
            pl.BlockSpec((nk, LANES), lambda i, s: (0, 0)),
            pl.BlockSpec((ce, d), lambda i, s: (jnp.minimum(s, nc - 1), 0)),
            pl.BlockSpec((1, d, ce), lambda i, s: (jnp.maximum(s - 1, 0), 0, 0)),
        ],
        out_specs=pl.BlockSpec((tm, d), lambda i, s: (i, 0)),
        out_shape=jax.ShapeDtypeStruct((n, d), F32),
        scratch_shapes=[
            pltpu.VMEM((d, tm), BF16),
            pltpu.VMEM((d, tm), F32),
            pltpu.VMEM((PEER_HEADS, tm, LANES), F32),
            pltpu.VMEM((ce, tm), F32),
            pltpu.VMEM((ce, tm), F32),
            pltpu.VMEM((ce, tm), BF16),
            pltpu.VMEM((PEER_HEADS, nk, tm), BF16),
            pltpu.VMEM((PEER_HEADS, nk, tm), BF16),
            pltpu.VMEM((PEER_HEADS, nk, tm), F32),
            pltpu.VMEM((PEER_HEADS, nk, tm), F32),
            pltpu.VMEM((nk, tm), F32),
            pltpu.VMEM((nk, tm), F32),
            pltpu.VMEM((nk, tm), F32),
            pltpu.VMEM((ncand, tm), F32),
            pltpu.VMEM((ncand, tm), F32),
            pltpu.VMEM((kk, 1, tm), F32),
            pltpu.VMEM((kk, 1, tm), F32),
            pltpu.VMEM((kk, 1, tm), F32),
        ],
        compiler_params=_cparams(("parallel", "arbitrary")),
        name="peer",
    )(x2d, g2, wq, k1p, k2p, u_bf, vt_bf)


def _tile128(g):
    return jnp.tile(g.reshape(1, -1), (1, LANES // g.shape[-1]))


def _layer(x3, pos_offset, hist, cache_k, cache_v, p, lam_init, *, tm, tt, tq, tm_peer, ce):
    bsz, t, d = x3.shape
    n = bsz * t
    x2d = x3.reshape(n, d)
    prompt = cache_k is None
    glu, q, k, kb, v, vb, gates = _in_proj(x2d, p["norm1_g"], p["w_in"], p["qg"], p["kg"],
                                           seq_len=t, pos_offset=pos_offset, tm=tm, k_transposed=prompt)
    dc = glu.shape[1]
    c, conv_state = _conv(glu.reshape(bsz, t, dc), hist, p["conv_dw_w"], p["conv_dw_b"],
                          p["conv_ln_g"], p["conv_ln_b"], tt=tt)
    lam_args = (p["lambda_q1"], p["lambda_k1"], p["lambda_q2"], p["lambda_k2"], p["subln_g"])
    r3 = lambda a: a.reshape(bsz, t, -1)
    if prompt:
        o = _attn_prompt(r3(q), kb, r3(vb), *lam_args, tq=tq, lam_init=lam_init)
        k = jnp.transpose(k.reshape(bsz, N_HEADS, 2, D_HEAD, t), (0, 4, 1, 2, 3))
    else:
        past = cache_k.shape[1]
        ckt = jnp.transpose(cache_k, (0, 2, 3, 4, 1)).reshape(bsz, -1, past)
        o = _attn_sample(r3(q), r3(kb), r3(vb), ckt, cache_v.reshape(bsz, past * N_HEADS, D_VHEAD), *lam_args,
                         lam_init=lam_init)
        k = k.reshape(bsz, t, N_HEADS, 2, D_HEAD)
    x_mid = _merge(c.reshape(n, dc), o.reshape(n, -1), gates, p["w_conv_out"], p["w_attn_out"],
                   p["w_out"], x2d, tm=tm)
    y = _peer(x_mid, p["norm2_g"], p["peer_wq"], p["k1p"], p["k2p"], p["peer_u"], p["peer_vt"],
              tm=tm_peer, ce=ce)
    return (y.reshape(bsz, t, d), k, v.reshape(bsz, t, N_HEADS, D_VHEAD), conv_state)


def kernel(x_prompt, x_sample, cache_attn_k, cache_attn_v, state_conv, norm1_g, w_in, conv_dw_w, conv_dw_b,
           conv_ln_g, conv_ln_b, w_conv_out, q_norm_g, k_norm_g, lambda_q1, lambda_k1, lambda_q2, lambda_k2,
           subln_g, w_attn_out, w_out, norm2_g, peer_wq, peer_k1, peer_k2, peer_u, peer_v):
    depth = w_in.shape[0]
    xp, xs = x_prompt, x_sample
    outs = [[] for _ in range(6)]
    row = lambda a: a.reshape(1, -1)

    def col_blocks(w, width=None):
        return _bf16_col_blocks(w, width or w.shape[1], CAST_ROWS)

    bf16 = lambda w: col_blocks(w)[0]
    d_model = x_prompt.shape[-1]
    ce = PEER_CHUNK
    for l in range(depth):
        half = peer_k1.shape[-1]
        p = dict(
            norm1_g=row(norm1_g[l]), w_in=col_blocks(w_in[l], d_model // 2),
            conv_dw_w=conv_dw_w[l], conv_dw_b=row(conv_dw_b[l]),
            conv_ln_g=row(conv_ln_g[l]), conv_ln_b=row(conv_ln_b[l]),
            w_conv_out=col_blocks(w_conv_out[l], d_model // 2),
            qg=_tile128(q_norm_g[l]), kg=_tile128(k_norm_g[l]),
            lambda_q1=row(lambda_q1[l]), lambda_k1=row(lambda_k1[l]),
            lambda_q2=row(lambda_q2[l]), lambda_k2=row(lambda_k2[l]),
            subln_g=row(subln_g[l]),
            w_attn_out=col_blocks(w_attn_out[l], d_model // 2), w_out=bf16(w_out[l]),
            norm2_g=row(norm2_g[l]), peer_wq=bf16(peer_wq[l]),
            k1p=jnp.pad(peer_k1[l], ((0, 0), (0, LANES - half))),
            k2p=jnp.pad(peer_k2[l], ((0, 0), (LANES - half, 0))),
            peer_u=bf16(peer_u[l]), peer_vt=_bf16_row_blocks_transposed(peer_v[l], ce),
        )
        li = _lambda_init(l)
        bp, tp, _ = xp.shape
        bs, ts, _ = xs.shape
        zero_hist = jnp.zeros((bp, CONV_PAD, state_conv.shape[-1]), xp.dtype)
        xp, kp, vp, cp = _layer(xp, 0, zero_hist, None, None, p, li,
                                tm=min(512, bp * tp), tt=min(256, tp), tq=min(512, tp),
                                tm_peer=min(512, bp * tp), ce=ce)
        hist_s = jnp.pad(state_conv[l], ((0, 0), (CONV_PAD - CONV_STATE, 0), (0, 0)))
        xs, ks, vs, cs = _layer(xs, cache_attn_k.shape[2], hist_s, cache_attn_k[l], cache_attn_v[l], p, li,
                                tm=min(256, bs * ts), tt=ts, tq=ts,
                                tm_peer=min(256, bs * ts), ce=ce)
        for lst, val in zip(outs, (kp, vp, cp, ks, vs, cs)):
            lst.append(val)
    kp, vp, cp, ks, vs, cs = (jnp.stack(o) for o in outs)
    return (xp, xs, kp, vp, cp, ks, vs, cs)
```

```python
import functools
import math

import jax
import jax.numpy as jnp
from jax import lax
from jax.experimental import pallas as pl
from jax.experimental.pallas import tpu as pltpu

F32 = jnp.float32
BF16 = jnp.bfloat16
I32 = jnp.int32

LANES = 128
SUBLANES = 8
CHUNK = 64
EPS = 1e-6
N_HEADS = 8
D_HEAD = 64
D_VHEAD = 2 * D_HEAD
CONV_WIDTH = 31
CONV_STATE = CONV_WIDTH - 1
CONV_PAD = 32
ROPE_THETA = 10000.0
NEG_INF = -1e30
PEER_HEADS = 8
PEER_TOPK = 16
PEER_CHUNK = 1024
CAST_ROWS = 512
VMEM_LIMIT = 60 * 1024 * 1024

NT_DIMS = (((1,), (1,)), ((), ()))
TN_DIMS = (((0,), (0,)), ((), ()))


def _cparams(sem):
    return pltpu.CompilerParams(dimension_semantics=sem, vmem_limit_bytes=VMEM_LIMIT)


def _lambda_init(layer):
    return 0.8 - 0.6 * math.exp(-0.3 * layer)


def _cast_kernel(w_ref, o_ref, *, transpose):
    w = w_ref[...]
    o_ref[0] = (w.T if transpose else w).astype(BF16)


def _bf16_col_blocks(w, width, rows):
    k, n = w.shape
    assert k % rows == 0 and n % width == 0
    return pl.pallas_call(
        functools.partial(_cast_kernel, transpose=False),
        grid=(n // width, k // rows),
        in_specs=[pl.BlockSpec((rows, width), lambda j, r: (r, j))],
        out_specs=pl.BlockSpec((1, rows, width), lambda j, r: (j, r, 0)),
        out_shape=jax.ShapeDtypeStruct((n // width, k, width), BF16),
        compiler_params=_cparams(("parallel", "parallel")),
        name="cast_col_blocks",
    )(w)


def _bf16_row_blocks_transposed(w, rows):
    e, d = w.shape
    assert e % rows == 0
    return pl.pallas_call(
        functools.partial(_cast_kernel, transpose=True),
        grid=(e // rows,),
        in_specs=[pl.BlockSpec((rows, d), lambda c: (c, 0))],
        out_specs=pl.BlockSpec((1, d, rows), lambda c: (c, 0, 0)),
        out_shape=jax.ShapeDtypeStruct((e // rows, d, rows), BF16),
        compiler_params=_cparams(("parallel",)),
        name="cast_row_blocks_t",
    )(w)


def _qk_norm_rope(z, g, cos, sin_signed):
    tm = z.shape[0]
    lane = lax.broadcasted_iota(I32, (tm, LANES), 1)
    lo = lane < D_HEAD
    first = (lane & (D_HEAD // 2)) == 0
    outs = []
    for c in range(z.shape[1] // LANES):
        zc = z[:, c * LANES:(c + 1) * LANES]
        zz = zc * zc
        s_lo = jnp.sum(jnp.where(lo, zz, 0.0), axis=-1, keepdims=True)
        s_hi = jnp.sum(jnp.where(lo, 0.0, zz), axis=-1, keepdims=True)
        r = jnp.where(lo, lax.rsqrt(s_lo * (1.0 / D_HEAD) + EPS), lax.rsqrt(s_hi * (1.0 / D_HEAD) + EPS))
        y = zc * r * g
        up = pltpu.roll(y, LANES - D_HEAD // 2, 1)
        dn = pltpu.roll(y, D_HEAD // 2, 1)
        outs.append(y * cos + jnp.where(first, up, dn) * sin_signed)
    return jnp.concatenate(outs, axis=1)


def _in_proj_kernel(x_ref, g1_ref, w_ref, qg_ref, kg_ref,
                    glu_ref, q_ref, k_ref, kb_ref, v_ref, vb_ref, gate_ref,
                    h_scr, za_scr, z0_scr, z1_scr, cos_scr, sin_scr,
                    *, seq_len, pos_offset, tm, k_transposed, n_blocks):
    i = pl.program_id(0)
    j = pl.program_id(1)
    z_scr = (z0_scr, z1_scr)

    def prologue():
        x = x_ref[...]
        ms = jnp.mean(x * x, axis=-1, keepdims=True)
        h_scr[...] = (x * lax.rsqrt(ms + EPS) * g1_ref[...]).astype(BF16)
        row = lax.broadcasted_iota(I32, (tm, LANES), 0) + i * tm
        pos = (row & (seq_len - 1)) + pos_offset
        lane = lax.broadcasted_iota(I32, (tm, LANES), 1)
        f = (lane & (D_HEAD // 2 - 1)).astype(F32)
        inv = jnp.power(jnp.float32(ROPE_THETA), -f / (D_HEAD // 2))
        ang = pos.astype(F32) * inv
        cos_scr[...] = jnp.cos(ang)
        s = jnp.sin(ang)
        sin_scr[...] = jnp.where((lane & (D_HEAD // 2)) == 0, -s, s)

    def finish(blk, z):
        if blk == 0:
            za_scr[...] = z
        elif blk == 1:
            glu_ref[...] = za_scr[...] * jax.nn.sigmoid(z)
        elif blk == 2:
            q = _qk_norm_rope(z, qg_ref[...], cos_scr[...], sin_scr[...])
            q_ref[...] = (q * (D_HEAD ** -0.5)).astype(BF16)
        elif blk == 3:
            k = _qk_norm_rope(z, kg_ref[...], cos_scr[...], sin_scr[...])
            if k_transposed:
                kt = k.T
                k_ref[0] = kt
                kb_ref[0] = kt.astype(BF16)
            else:
                k_ref[...] = k
                kb_ref[...] = k.astype(BF16)
        elif blk == 4:
            v_ref[...] = z
            vb_ref[...] = z.astype(BF16)
        else:
            gate_ref[...] = jax.nn.sigmoid(z).astype(BF16)

    for step in range(n_blocks + 1):
        @pl.when(j == step)
        def _(step=step):
            if step == 0:
                prologue()
            if step >= 1:
                finish(step - 1, z_scr[(step - 1) % 2][...])
            if step < n_blocks:
                z_scr[step % 2][...] = jnp.dot(h_scr[...], w_ref[0], preferred_element_type=F32)


def _in_proj(x2d, g1, w_in_blocks, qg, kg, *, seq_len, pos_offset, tm, k_transposed):
    n, d = x2d.shape
    nj, _, tn = w_in_blocks.shape
    assert n % tm == 0 and tn == d // 2 and nj == 9
    assert seq_len & (seq_len - 1) == 0
    blk = lambda: pl.BlockSpec((tm, tn), lambda i, j: (i, 0))
    if k_transposed:
        assert seq_len % tm == 0
        per = seq_len // tm
        k_shape = (n // seq_len, tn, seq_len)
        kblk = lambda: pl.BlockSpec((1, tn, tm), lambda i, j: (i // per, 0, i % per))
    else:
        k_shape = (n, tn)
        kblk = blk
    out_shape = (
        jax.ShapeDtypeStruct((n, tn), F32),
        jax.ShapeDtypeStruct((n, tn), BF16),
        jax.ShapeDtypeStruct(k_shape, F32),
        jax.ShapeDtypeStruct(k_shape, BF16),
        jax.ShapeDtypeStruct((n, tn), F32),
        jax.ShapeDtypeStruct((n, tn), BF16),
        jax.ShapeDtypeStruct((n, 4 * tn), BF16),
    )
    return pl.pallas_call(
        functools.partial(_in_proj_kernel, seq_len=seq_len, pos_offset=pos_offset, tm=tm,
                          k_transposed=k_transposed, n_blocks=nj),
        grid=(n // tm, nj + 1),
        in_specs=[
            pl.BlockSpec((tm, d), lambda i, j: (i, 0)),
            pl.BlockSpec((1, d), lambda i, j: (0, 0)),
            pl.BlockSpec((1, d, tn), lambda i, j: (jnp.minimum(j, nj - 1), 0, 0)),
            pl.BlockSpec((1, LANES), lambda i, j: (0, 0)),
            pl.BlockSpec((1, LANES), lambda i, j: (0, 0)),
        ],
        out_specs=(blk(), blk(), kblk(), kblk(), blk(), blk(),
                   pl.BlockSpec((tm, tn), lambda i, j: (i, jnp.maximum(j - 6, 0)))),
        out_shape=out_shape,
        scratch_shapes=[
            pltpu.VMEM((tm, d), BF16),
            pltpu.VMEM((tm, tn), F32),
            pltpu.VMEM((tm, tn), F32),
            pltpu.VMEM((tm, tn), F32),
            pltpu.VMEM((tm, LANES), F32),
            pltpu.VMEM((tm, LANES), F32),
        ],
        compiler_params=_cparams(("parallel", "arbitrary")),
        name="in_proj",
    )(x2d, g1, w_in_blocks, qg, kg)


CONV_ROWS = 64


def _conv_kernel(cur_ref, prev_ref, st_ref, w_ref, b_ref, lg_ref, lb_ref,
                 c_ref, so_ref, buf, shifted, *, tt):
    ti = pl.program_id(1)
    dc = cur_ref.shape[2]

    @pl.when(ti == 0)
    def _():
        buf[0:CONV_PAD, :] = st_ref[0]

    @pl.when(ti > 0)
    def _():
        buf[0:CONV_PAD, :] = prev_ref[0]

    buf[CONV_PAD:CONV_PAD + tt, :] = cur_ref[0]
    lead = CONV_PAD - CONV_STATE
    rows = min(CONV_ROWS, tt)
    span = shifted.shape[1]
    for p in range(1, SUBLANES):
        shifted[p - 1] = buf[p:p + span, :]

    for rs in range(tt // rows):
        parts = []
        for cs in range(dc // LANES):
            sl = slice(cs * LANES, (cs + 1) * LANES)
            acc = jnp.zeros((rows, LANES), F32)
            for j in range(CONV_WIDTH):
                phase = (lead + j) % SUBLANES
                r0 = lead + j - phase + rs * rows
                src = buf if phase == 0 else shifted.at[phase - 1]
                acc = acc + src[r0:r0 + rows, sl] * w_ref[j:j + 1, sl]
            parts.append(acc + b_ref[:, sl])
        c = jnp.concatenate(parts, axis=1)
        mu = jnp.mean(c, axis=-1, keepdims=True)
        var = jnp.mean(jnp.square(c - mu), axis=-1, keepdims=True)
        y = (c - mu) * lax.rsqrt(var + EPS) * lg_ref[...] + lb_ref[...]
        c_ref[0, rs * rows:(rs + 1) * rows, :] = (y * jax.nn.sigmoid(y)).astype(BF16)

    @pl.when(ti == pl.num_programs(1) - 1)
    def _():
        so_ref[0] = buf[lead + tt:lead + tt + CONV_STATE, :]


def _conv(glu3d, hist, w, b, lg, lb, *, tt):
    bsz, t, dc = glu3d.shape
    assert t % tt == 0 and tt % min(CONV_ROWS, tt) == 0 and tt % CONV_PAD == 0
    per = tt // CONV_PAD
    return pl.pallas_call(
        functools.partial(_conv_kernel, tt=tt),
        grid=(bsz, t // tt),
        in_specs=[
            pl.BlockSpec((1, tt, dc), lambda bi, ti: (bi, ti, 0)),
            pl.BlockSpec((1, CONV_PAD, dc), lambda bi, ti: (bi, jnp.maximum(ti * per - 1, 0), 0)),
            pl.BlockSpec((1, CONV_PAD, dc), lambda bi, ti: (bi, 0, 0)),
            pl.BlockSpec((CONV_WIDTH, dc), lambda bi, ti: (0, 0)),
            pl.BlockSpec((1, dc), lambda bi, ti: (0, 0)),
            pl.BlockSpec((1, dc), lambda bi, ti: (0, 0)),
            pl.BlockSpec((1, dc), lambda bi, ti: (0, 0)),
        ],
        out_specs=(
            pl.BlockSpec((1, tt, dc), lambda bi, ti: (bi, ti, 0)),
            pl.BlockSpec((1, CONV_STATE, dc), lambda bi, ti: (bi, 0, 0)),
        ),
        out_shape=(
            jax.ShapeDtypeStruct((bsz, t, dc), BF16),
            jax.ShapeDtypeStruct((bsz, CONV_STATE, dc), F32),
        ),
        scratch_shapes=[
            pltpu.VMEM((CONV_PAD + tt, dc), F32),
            pltpu.VMEM((SUBLANES - 1, tt + CONV_PAD - SUBLANES, dc), F32),
        ],
        compiler_params=_cparams(("parallel", "arbitrary")),
        name="conv_branch",
    )(glu3d, glu3d, hist, w, b, lg, lb)


def _lambda_value(lq1_ref, lk1_ref, lq2_ref, lk2_ref, lam_init):
    a = jnp.sum(lq1_ref[...] * lk1_ref[...], axis=-1, keepdims=True)
    b = jnp.sum(lq2_ref[...] * lk2_ref[...], axis=-1, keepdims=True)
    return jnp.exp(a) - jnp.exp(b) + lam_init


def _head_finish(o1, o2, lam, sg, lam_init):
    o = o1 - lam * o2
    ms = jnp.mean(o * o, axis=-1, keepdims=True)
    return (o * lax.rsqrt(ms + EPS) * sg) * (1.0 - lam_init)


def _split_heads(q):
    lane = lax.broadcasted_iota(I32, q.shape, 1)
    zero = jnp.zeros_like(q)
    return jnp.where(lane < D_HEAD, q, zero), jnp.where(lane >= D_HEAD, q, zero)


ATTN_FULL_FULL, ATTN_FULL_DIAG, ATTN_DIAG = 0, 1, 2


def _attn_blocks(qcs, blocks, state):
    m, acc = state
    scores = []
    for kt, _, diagonal in blocks:
        ss = [jnp.dot(qc, kt, preferred_element_type=F32) for qc in qcs]
        if diagonal:
            tq, tk = ss[0].shape
            row_chunk = lax.broadcasted_iota(I32, (tq, tk), 0) // CHUNK
            col_chunk = lax.broadcasted_iota(I32, (tq, tk), 1) // CHUNK
            mask = col_chunk <= row_chunk
            ss = [jnp.where(mask, s, NEG_INF) for s in ss]
        scores.append(ss)
    for ss, (_, v, _) in zip(scores, blocks):
        tk = ss[0].shape[1]
        v_ext = jnp.concatenate([v, jnp.ones_like(v)], axis=1)
        m_new = [jnp.maximum(m[c], jnp.max(ss[c], axis=-1, keepdims=True)) for c in range(2)]
        ps = [jnp.exp(ss[c] - jnp.concatenate([m_new[c]] * (tk // LANES), axis=1)).astype(BF16) for c in range(2)]
        pvs = [jnp.dot(ps[c], v_ext, preferred_element_type=F32) for c in range(2)]
        alphas = [jnp.exp(m[c] - m_new[c]) for c in range(2)]
        acc = [jnp.concatenate([alphas[c], alphas[c]], axis=1) * acc[c] + pvs[c] for c in range(2)]
        m = m_new
    return m, acc


def _attn_prompt_kernel(qi_tab, ka_tab, kb_tab, kind_tab, q_ref, kta_ref, ktb_ref, va_ref, vb_ref,
                        lq1_ref, lk1_ref, lq2_ref, lk2_ref, sg_ref, o_ref, m_scr, acc_scr, *, lam_init):
    s_id = pl.program_id(2)
    kind = kind_tab[s_id]

    @pl.when(ka_tab[s_id] == 0)
    def _():
        m_scr[...] = jnp.full(m_scr.shape, -jnp.inf, F32)
        acc_scr[...] = jnp.zeros(acc_scr.shape, F32)

    def load():
        return _split_heads(q_ref[0]), ([m_scr[c] for c in range(2)], [acc_scr[c] for c in range(2)])

    def store(state):
        for c in range(2):
            m_scr[c] = state[0][c]
            acc_scr[c] = state[1][c]

    def finish(state):
        lam = _lambda_value(lq1_ref, lk1_ref, lq2_ref, lk2_ref, lam_init)
        a1, a2 = state[1]
        y = _head_finish(a1[:, :D_VHEAD] / a1[:, D_VHEAD:], a2[:, :D_VHEAD] / a2[:, D_VHEAD:],
                         lam, sg_ref[...], lam_init)
        o_ref[0] = y.astype(BF16)

    @pl.when(kind == ATTN_FULL_FULL)
    def _():
        qcs, state = load()
        store(_attn_blocks(qcs, [(kta_ref[0], va_ref[0], False), (ktb_ref[0], vb_ref[0], False)], state))

    @pl.when(kind == ATTN_FULL_DIAG)
    def _():
        qcs, state = load()
        finish(_attn_blocks(qcs, [(kta_ref[0], va_ref[0], False), (ktb_ref[0], vb_ref[0], True)], state))

    @pl.when(kind == ATTN_DIAG)
    def _():
        qcs, state = load()
        finish(_attn_blocks(qcs, [(kta_ref[0], va_ref[0], True)], state))


def _attn_steps(nq):
    steps = []
    for qi in range(nq):
        ki = 0
        while ki + 1 < qi:
            steps.append((qi, ki, ki + 1, ATTN_FULL_FULL))
            ki += 2
        if ki + 1 == qi:
            steps.append((qi, ki, qi, ATTN_FULL_DIAG))
        else:
            steps.append((qi, qi, qi, ATTN_DIAG))
    return steps


def _attn_prompt(q3, kt3, v3, lq1, lk1, lq2, lk2, sg, *, tq, lam_init):
    bsz, t, da = q3.shape
    nq = t // tq
    assert t % tq == 0 and tq % CHUNK == 0 and tq % LANES == 0
    steps = _attn_steps(nq)
    tabs = [jnp.asarray([st[i] for st in steps], I32) for i in range(4)]
    vec = lambda n: pl.BlockSpec((1, n), lambda b, h, s, *_: (0, 0))
    grid_spec = pltpu.PrefetchScalarGridSpec(
        num_scalar_prefetch=4,
        grid=(bsz, N_HEADS, len(steps)),
        in_specs=[
            pl.BlockSpec((1, tq, LANES), lambda b, h, s, qt, ka, kb, kd: (b, qt[s], h)),
            pl.BlockSpec((1, LANES, tq), lambda b, h, s, qt, ka, kb, kd: (b, h, ka[s])),
            pl.BlockSpec((1, LANES, tq), lambda b, h, s, qt, ka, kb, kd: (b, h, kb[s])),
            pl.BlockSpec((1, tq, LANES), lambda b, h, s, qt, ka, kb, kd: (b, ka[s], h)),
            pl.BlockSpec((1, tq, LANES), lambda b, h, s, qt, ka, kb, kd: (b, kb[s], h)),
            vec(D_HEAD), vec(D_HEAD), vec(D_HEAD), vec(D_HEAD), vec(D_VHEAD),
        ],
        out_specs=pl.BlockSpec((1, tq, LANES), lambda b, h, s, qt, ka, kb, kd: (b, qt[s], h)),
        scratch_shapes=[
            pltpu.VMEM((2, tq, LANES), F32),
            pltpu.VMEM((2, tq, 2 * D_VHEAD), F32),
        ],
    )
    return pl.pallas_call(
        functools.partial(_attn_prompt_kernel, lam_init=lam_init),
        grid_spec=grid_spec,
        out_shape=jax.ShapeDtypeStruct((bsz, t, da), BF16),
        compiler_params=_cparams(("parallel", "parallel", "arbitrary")),
        name="attn_prompt",
    )(*tabs, q3, kt3, kt3, v3, v3, lq1, lk1, lq2, lk2, sg)


def _attn_sample_kernel(q_ref, kn_ref, vn_ref, ck_ref, cv_ref, lq1_ref, lk1_ref, lq2_ref, lk2_ref, sg_ref,
                        o_ref, *, lam_init):
    past = ck_ref.shape[2]
    lam = _lambda_value(lq1_ref, lk1_ref, lq2_ref, lk2_ref, lam_init)
    for h in range(N_HEADS):
        sl = slice(h * LANES, (h + 1) * LANES)
        q = q_ref[0, :, sl]
        kn = kn_ref[0, :, sl]
        vn = vn_ref[0, :, sl]
        kpt = ck_ref[0, sl, :].astype(BF16)
        vp = cv_ref[0, pl.ds(h, past, stride=N_HEADS), :].astype(BF16)
        outs = []
        for qc in _split_heads(q):
            sp = jnp.dot(qc, kpt, preferred_element_type=F32)
            sn = lax.dot_general(qc, kn, NT_DIMS, preferred_element_type=F32)
            m = jnp.maximum(jnp.max(sp, axis=-1, keepdims=True), jnp.max(sn, axis=-1, keepdims=True))
            pp = jnp.exp(sp - m)
            pn = jnp.exp(sn - m)
            l = jnp.sum(pp, axis=-1, keepdims=True) + jnp.sum(pn, axis=-1, keepdims=True)
            o = (jnp.dot(pp.astype(BF16), vp, preferred_element_type=F32)
                 + jnp.dot(pn.astype(BF16), vn, preferred_element_type=F32))
            outs.append(o / l)
        o_ref[0, :, sl] = _head_finish(outs[0], outs[1], lam, sg_ref[...], lam_init).astype(BF16)


def _attn_sample(q3, kn3, vn3, ckt3, cv3, lq1, lk1, lq2, lk2, sg, *, lam_init):
    bsz, t, da = q3.shape
    past = ckt3.shape[2]
    vec = lambda n: pl.BlockSpec((1, n), lambda b: (0, 0))
    new = lambda: pl.BlockSpec((1, t, da), lambda b: (b, 0, 0))
    return pl.pallas_call(
        functools.partial(_attn_sample_kernel, lam_init=lam_init),
        grid=(bsz,),
        in_specs=[new(), new(), new(),
                  pl.BlockSpec((1, da, past), lambda b: (b, 0, 0)),
                  pl.BlockSpec((1, past * N_HEADS, D_VHEAD), lambda b: (b, 0, 0)),
                  vec(D_HEAD), vec(D_HEAD), vec(D_HEAD), vec(D_HEAD), vec(D_VHEAD)],
        out_specs=new(),
        out_shape=jax.ShapeDtypeStruct((bsz, t, da), BF16),
        compiler_params=_cparams(("parallel",)),
        name="attn_sample",
    )(q3, kn3, vn3, ckt3, cv3, lq1, lk1, lq2, lk2, sg)


def _merge_kernel(c_ref, o_ref, gc_ref, ga_ref, wco_ref, wao_ref, wout_ref, x_ref, y_ref):
    j = pl.program_id(1)

    @pl.when(j == 0)
    def _():
        y_ref[...] = x_ref[...]

    co = jnp.dot(c_ref[...], wco_ref[0], preferred_element_type=F32)
    ao = jnp.dot(o_ref[...], wao_ref[0], preferred_element_type=F32)
    merged = gc_ref[...].astype(F32) * co + ga_ref[...].astype(F32) * ao
    y_ref[...] += jnp.dot(merged.astype(BF16), wout_ref[...], preferred_element_type=F32)


def _merge(c2d, o2d, gates, wco, wao, wout, x2d, *, tm):
    n, d = x2d.shape
    dh = d // 2
    assert n % tm == 0
    return pl.pallas_call(
        _merge_kernel,
        grid=(n // tm, 2),
        in_specs=[
            pl.BlockSpec((tm, dh), lambda i, j: (i, 0)),
            pl.BlockSpec((tm, dh), lambda i, j: (i, 0)),
            pl.BlockSpec((tm, dh), lambda i, j: (i, j)),
            pl.BlockSpec((tm, dh), lambda i, j: (i, 2 + j)),
            pl.BlockSpec((1, dh, dh), lambda i, j: (j, 0, 0)),
            pl.BlockSpec((1, dh, dh), lambda i, j: (j, 0, 0)),
            pl.BlockSpec((dh, d), lambda i, j: (j, 0)),
            pl.BlockSpec((tm, d), lambda i, j: (i, 0)),
        ],
        out_specs=pl.BlockSpec((tm, d), lambda i, j: (i, 0)),
        out_shape=jax.ShapeDtypeStruct((n, d), F32),
        compiler_params=_cparams(("parallel", "arbitrary")),
        name="merge_out_proj",
    )(c2d, o2d, gates, gates, wco, wao, wout, x2d)


def _topk_rows(scores, cur_ref, rank_ref, val_ref, k):
    rows = scores.shape[0]

    cur_ref[...] = scores
    rank_ref[...] = jnp.full(rank_ref.shape, float(k), F32)

    def fast(j, m):
        cur = cur_ref[...]
        hit = cur == m
        rank_ref[...] = jnp.where(hit, lax.convert_element_type(j, F32), rank_ref[...])
        nxt = jnp.where(hit, -jnp.inf, cur)
        cur_ref[...] = nxt
        val_ref[j] = m
        return jnp.max(nxt, axis=0, keepdims=True)

    lax.fori_loop(0, k, fast, jnp.max(scores, axis=0, keepdims=True))
    ranked = jnp.sum(jnp.where(rank_ref[...] < float(k), 1.0, 0.0), axis=0, keepdims=True)
    tied = jnp.max(ranked) > float(k)

    @pl.when(tied)
    def _():
        iota = lax.broadcasted_iota(I32, (rows, scores.shape[1]), 0).astype(F32)
        cur_ref[...] = scores
        rank_ref[...] = jnp.full(rank_ref.shape, float(k), F32)

        def exact(j, carry):
            cur = cur_ref[...]
            m = jnp.max(cur, axis=0, keepdims=True)
            idx = jnp.min(jnp.where(cur == m, iota, float(rows)), axis=0, keepdims=True)
            hit = iota == idx
            rank_ref[...] = jnp.where(hit, lax.convert_element_type(j, F32), rank_ref[...])
            cur_ref[...] = jnp.where(hit, -jnp.inf, cur)
            val_ref[j] = m
            return carry

        lax.fori_loop(0, k, exact, 0)


def _cand_layout(kk):
    counts = [kk // (j1 + 1) for j1 in range(kk)]
    starts = [sum(counts[:j1]) for j1 in range(kk)]
    return counts, starts, sum(counts)


def _peer_route(x_ref, g2_ref, wq_ref, k1_ref, k2_ref, yt_scr, ht_scr, q_scr, e2_scr, rank2_scr, c_scr, n_scr,
                cur_scr, rank_scr, rank1_scr, cand_scr, crank_scr, val1_scr, val2_scr, valc_scr, *, tm):
    nk = k1_ref.shape[0]
    kk = PEER_TOPK
    counts, starts, ncand = _cand_layout(kk)
    x = x_ref[...]
    yt_scr[...] = jnp.zeros(yt_scr.shape, F32)
    ms = jnp.mean(x * x, axis=-1, keepdims=True)
    hf = x * lax.rsqrt(ms + EPS) * g2_ref[...]
    ht_scr[...] = hf.T.astype(BF16)
    q = jnp.dot(hf.astype(BF16), wq_ref[...], preferred_element_type=F32)
    for hd in range(PEER_HEADS):
        q_scr[hd] = q[:, hd * LANES:(hd + 1) * LANES]

    def head_body(hd, carry):
        qh = q_scr[hd]
        s1 = lax.dot_general(k1_ref[...], qh, NT_DIMS, preferred_element_type=F32)
        s2 = lax.dot_general(k2_ref[...], qh, NT_DIMS, preferred_element_type=F32)
        _topk_rows(s1, cur_scr, rank1_scr, val1_scr, kk)
        _topk_rows(s2, cur_scr, rank_scr, val2_scr, kk)
        rank2_scr[hd] = rank_scr[...].astype(BF16)
        v1max = val1_scr[0]
        v2max = val2_scr[0]
        v2all = jnp.concatenate([val2_scr[j2] for j2 in range(kk)], axis=0)
        pieces = [val1_scr[j1] + v2all[:counts[j1], :] for j1 in range(kk)]
        pieces.append(jnp.full((cand_scr.shape[0] - ncand, tm), -jnp.inf, F32))
        cand = jnp.concatenate(pieces, axis=0)
        _topk_rows(cand, cand_scr, crank_scr, valc_scr, kk)
        picked = crank_scr[...] < float(kk)
        z = jnp.sum(jnp.where(picked, jnp.exp(cand - valc_scr[0]), 0.0), axis=0, keepdims=True)
        crow = lax.broadcasted_iota(I32, cand.shape, 0)
        rank1 = rank1_scr[...]
        n_by = jnp.zeros((nk, tm), F32)
        for j1 in range(kk):
            in_row = (crow >= starts[j1]) & (crow < starts[j1] + counts[j1])
            n_j1 = jnp.sum(jnp.where(picked & in_row, 1.0, 0.0), axis=0, keepdims=True)
            n_by = n_by + jnp.where(rank1 == float(j1), n_j1, 0.0)
        n_scr[hd] = n_by
        c_scr[hd] = jnp.exp(s1 - v1max) / z
        e2_scr[hd] = jnp.exp(s2 - v2max).astype(BF16)
        return carry

    lax.fori_loop(0, PEER_HEADS, head_body, 0)


def _peer_scores(u_ref, ht_scr, at_scr):
    at_scr[...] = jnp.dot(u_ref[...], ht_scr[...], preferred_element_type=F32)


PEER_KEYS_PER_DOT = 2


def _peer_gates(chunk, at_scr, wt_scr, e2_scr, rank2_scr, c_scr, n_scr, *, tm, ce, nk):
    for blk in range(ce // nk):
        r = chunk * (ce // nk) + blk
        acc = jnp.zeros((nk, tm), BF16)
        for hd in range(PEER_HEADS):
            n_row = n_scr[hd, pl.ds(r, 1), :].astype(BF16)
            c_row = c_scr[hd, pl.ds(r, 1), :].astype(BF16)
            acc = acc + jnp.where(rank2_scr[hd] < n_row, e2_scr[hd] * c_row, 0)
        a = at_scr[blk * nk:(blk + 1) * nk, :]
        gelu = 0.5 * a * (1.0 + lax.erf(a * (2.0 ** -0.5)))
        wt_scr[blk * nk:(blk + 1) * nk, :] = acc * gelu.astype(BF16)


def _peer_combine(wt_scr, vt_ref, yt_scr, *, nk):
    per = PEER_KEYS_PER_DOT * nk
    total = None
    for g in range(wt_scr.shape[0] // per):
        rows = slice(g * per, (g + 1) * per)
        d = jnp.dot(vt_ref[0, :, rows], wt_scr[rows, :], preferred_element_type=F32)
        total = d if total is None else total + d
    yt_scr[...] += total


def _peer_kernel(x_ref, g2_ref, wq_ref, k1_ref, k2_ref, u_ref, vt_ref, y_ref,
                 ht_scr, yt_scr, q_scr, at0_scr, at1_scr, wt_scr, e2_scr, rank2_scr, c_scr, n_scr,
                 cur_scr, rank_scr, rank1_scr, cand_scr, crank_scr, val1_scr, val2_scr, valc_scr,
                 *, tm, ce):
    s = pl.program_id(1)
    last = pl.num_programs(1) - 1
    nk = k1_ref.shape[0]
    even = s % 2 == 0

    def step(score_into, mix_from):
        if mix_from is not None:
            _peer_gates(s - 1, mix_from, wt_scr, e2_scr, rank2_scr, c_scr, n_scr, tm=tm, ce=ce, nk=nk)
        if score_into is not None:
            _peer_scores(u_ref, ht_scr, score_into)
        if mix_from is not None:
            _peer_combine(wt_scr, vt_ref, yt_scr, nk=nk)

    @pl.when(s == 0)
    def _():
        _peer_route(x_ref, g2_ref, wq_ref, k1_ref, k2_ref, yt_scr, ht_scr, q_scr, e2_scr, rank2_scr, c_scr, n_scr,
                    cur_scr, rank_scr, rank1_scr, cand_scr, crank_scr, val1_scr, val2_scr, valc_scr, tm=tm)
        step(at0_scr, None)

    @pl.when((s > 0) & (s < last) & even)
    def _():
        step(at0_scr, at1_scr)

    @pl.when((s < last) & jnp.logical_not(even))
    def _():
        step(at1_scr, at0_scr)

    @pl.when(s == last)
    def _():
        step(None, at1_scr)
        y_ref[...] = x_ref[...] + yt_scr[...].T


def _peer(x2d, g2, wq, k1p, k2p, u_bf, vt_bf, *, tm, ce):
    n, d = x2d.shape
    ne = u_bf.shape[0]
    nk = k1p.shape[0]
    kk = PEER_TOPK
    nc = ne // ce
    ncand = -(-_cand_layout(kk)[2] // 8) * 8
    assert n % tm == 0 and ne % ce == 0 and ce % (2 * nk) == 0 and ne == nk * nk and nk == LANES
    assert nc % 2 == 0
    return pl.pallas_call(
        functools.partial(_peer_kernel, tm=tm, ce=ce),
        grid=(n // tm, nc + 1),
        in_specs=[
            pl.BlockSpec((tm, d), lambda i, s: (i, 0), pipeline_mode=pl.Buffered(1)),
            pl.BlockSpec((1, d), lambda i, s: (0, 0)),
            pl.BlockSpec(wq.shape, lambda i, s: (0, 0), pipeline_mode=pl.Buffered(1)),
            pl.BlockSpec((nk, LANES), lambda i, s: (0, 0)),
            pl.BlockSpec((nk, LANES), lambda i, s: (0, 0)),
            pl.BlockSpec((ce, d), lambda i, s: (jnp.minimum(s, nc - 1), 0)),
            pl.BlockSpec((1, d, ce), lambda i, s: (jnp.maximum(s - 1, 0), 0, 0)),
        ],
        out_specs=pl.BlockSpec((tm, d), lambda i, s: (i, 0)),
        out_shape=jax.ShapeDtypeStruct((n, d), F32),
        scratch_shapes=[
            pltpu.VMEM((d, tm), BF16),
            pltpu.VMEM((d, tm), F32),
            pltpu.VMEM((PEER_HEADS, tm, LANES), F32),
            pltpu.VMEM((ce, tm), F32),
            pltpu.VMEM((ce, tm), F32),
            pltpu.VMEM((ce, tm), BF16),
            pltpu.VMEM((PEER_HEADS, nk, tm), BF16),
            pltpu.VMEM((PEER_HEADS, nk, tm), BF16),
            pltpu.VMEM((PEER_HEADS, nk, tm), F32),
            pltpu.VMEM((PEER_HEADS, nk, tm), F32),
            pltpu.VMEM((nk, tm), F32),
            pltpu.VMEM((nk, tm), F32),
            pltpu.VMEM((nk, tm), F32),
            pltpu.VMEM((ncand, tm), F32),
            pltpu.VMEM((ncand, tm), F32),
            pltpu.VMEM((kk, 1, tm), F32),
            pltpu.VMEM((kk, 1, tm), F32),
            pltpu.VMEM((kk, 1, tm), F32),
        ],
        compiler_params=_cparams(("parallel", "arbitrary")),
        name="peer",
    )(x2d, g2, wq, k1p, k2p, u_bf, vt_bf)


def _tile128(g):
    return jnp.tile(g.reshape(1, -1), (1, LANES // g.shape[-1]))


def _layer(x3, pos_offset, hist, cache_k, cache_v, p, lam_init, *, tm, tt, tq, tm_peer, ce):
    bsz, t, d = x3.shape
    n = bsz * t
    x2d = x3.reshape(n, d)
    prompt = cache_k is None
    glu, q, k, kb, v, vb, gates = _in_proj(x2d, p["norm1_g"], p["w_in"], p["qg"], p["kg"],
                                           seq_len=t, pos_offset=pos_offset, tm=tm, k_transposed=prompt)
    dc = glu.shape[1]
    c, conv_state = _conv(glu.reshape(bsz, t, dc), hist, p["conv_dw_w"], p["conv_dw_b"],
                          p["conv_ln_g"], p["conv_ln_b"], tt=tt)
    lam_args = (p["lambda_q1"], p["lambda_k1"], p["lambda_q2"], p["lambda_k2"], p["subln_g"])
    r3 = lambda a: a.reshape(bsz, t, -1)
    if prompt:
        o = _attn_prompt(r3(q), kb, r3(vb), *lam_args, tq=tq, lam_init=lam_init)
        k = jnp.transpose(k.reshape(bsz, N_HEADS, 2, D_HEAD, t), (0, 4, 1, 2, 3))
    else:
        past = cache_k.shape[1]
        ckt = jnp.transpose(cache_k, (0, 2, 3, 4, 1)).reshape(bsz, -1, past)
        o = _attn_sample(r3(q), r3(kb), r3(vb), ckt, cache_v.reshape(bsz, past * N_HEADS, D_VHEAD), *lam_args,
                         lam_init=lam_init)
        k = k.reshape(bsz, t, N_HEADS, 2, D_HEAD)
    x_mid = _merge(c.reshape(n, dc), o.reshape(n, -1), gates, p["w_conv_out"], p["w_attn_out"],
                   p["w_out"], x2d, tm=tm)
    y = _peer(x_mid, p["norm2_g"], p["peer_wq"], p["k1p"], p["k2p"], p["peer_u"], p["peer_vt"],
              tm=tm_peer, ce=ce)
    return (y.reshape(bsz, t, d), k, v.reshape(bsz, t, N_HEADS, D_VHEAD), conv_state)


def kernel(x_prompt, x_sample, cache_attn_k, cache_attn_v, state_conv, norm1_g, w_in, conv_dw_w, conv_dw_b,
           conv_ln_g, conv_ln_b, w_conv_out, q_norm_g, k_norm_g, lambda_q1, lambda_k1, lambda_q2, lambda_k2,
           subln_g, w_attn_out, w_out, norm2_g, peer_wq, peer_k1, peer_k2, peer_u, peer_v):
    depth = w_in.shape[0]
    xp, xs = x_prompt, x_sample
    outs = [[] for _ in range(6)]
    row = lambda a: a.reshape(1, -1)

    def col_blocks(w, width=None):
        return _bf16_col_blocks(w, width or w.shape[1], CAST_ROWS)

    bf16 = lambda w: col_blocks(w)[0]
    d_model = x_prompt.shape[-1]
    ce = PEER_CHUNK
    for l in range(depth):
        half = peer_k1.shape[-1]
        p = dict(
            norm1_g=row(norm1_g[l]), w_in=col_blocks(w_in[l], d_model // 2),
            conv_dw_w=conv_dw_w[l], conv_dw_b=row(conv_dw_b[l]),
            conv_ln_g=row(conv_ln_g[l]), conv_ln_b=row(conv_ln_b[l]),
            w_conv_out=col_blocks(w_conv_out[l], d_model // 2),
            qg=_tile128(q_norm_g[l]), kg=_tile128(k_norm_g[l]),
            lambda_q1=row(lambda_q1[l]), lambda_k1=row(lambda_k1[l]),
            lambda_q2=row(lambda_q2[l]), lambda_k2=row(lambda_k2[l]),
            subln_g=row(subln_g[l]),
            w_attn_out=col_blocks(w_attn_out[l], d_model // 2), w_out=bf16(w_out[l]),
            norm2_g=row(norm2_g[l]), peer_wq=bf16(peer_wq[l]),
            k1p=jnp.pad(peer_k1[l], ((0, 0), (0, LANES - half))),
            k2p=jnp.pad(peer_k2[l], ((0, 0), (LANES - half, 0))),
            peer_u=bf16(peer_u[l]), peer_vt=_bf16_row_blocks_transposed(peer_v[l], ce),
        )
        li = _lambda_init(l)
        bp, tp, _ = xp.shape
        bs, ts, _ = xs.shape
        zero_hist = jnp.zeros((bp, CONV_PAD, state_conv.shape[-1]), xp.dtype)
        xp, kp, vp, cp = _layer(xp, 0, zero_hist, None, None, p, li,
                                tm=min(512, bp * tp), tt=min(256, tp), tq=min(512, tp),
                                tm_peer=min(512, bp * tp), ce=ce)
        hist_s = jnp.pad(state_conv[l], ((0, 0), (CONV_PAD - CONV_STATE, 0), (0, 0)))
        xs, ks, vs, cs = _layer(xs, cache_attn_k.shape[2], hist_s, cache_attn_k[l], cache_attn_v[l], p, li,
                                tm=min(256, bs * ts), tt=ts, tq=ts,
                                tm_peer=min(256, bs * ts), ce=ce)
        for lst, val in zip(outs, (kp, vp, cp, ks, vs, cs)):
            lst.append(val)
    kp, vp, cp, ks, vs, cs = (jnp.stack(o) for o in outs)
    return (xp, xs, kp, vp, cp, ks, vs, cs)
```

```python
import functools
import math

import jax
import jax.numpy as jnp
from jax import lax
from jax.experimental import pallas as pl
from jax.experimental.pallas import tpu as pltpu

F32 = jnp.float32
BF16 = jnp.bfloat16
I32 = jnp.int32

LANES = 128
SUBLANES = 8
CHUNK = 64
EPS = 1e-6
N_HEADS = 8
D_HEAD = 64
D_VHEAD = 2 * D_HEAD
CONV_WIDTH = 31
CONV_STATE = CONV_WIDTH - 1
CONV_PAD = 32
ROPE_THETA = 10000.0
LOG2_E = math.log2(math.e)
NEG_INF = -1e30
PEER_HEADS = 8
PEER_TOPK = 16
PEER_CHUNK = 1024
CAST_ROWS = 512
VMEM_LIMIT = 60 * 1024 * 1024

NT_DIMS = (((1,), (1,)), ((), ()))
TN_DIMS = (((0,), (0,)), ((), ()))


def _cparams(sem):
    return pltpu.CompilerParams(dimension_semantics=sem, vmem_limit_bytes=VMEM_LIMIT)


def _lambda_init(layer):
    return 0.8 - 0.6 * math.exp(-0.3 * layer)


def _cast_kernel(w_ref, o_ref):
    o_ref[0] = w_ref[...].astype(BF16)


def _bf16_col_blocks(w, width, rows):
    k, n = w.shape
    assert k % rows == 0 and n % width == 0
    return pl.pallas_call(
        _cast_kernel,
        grid=(n // width, k // rows),
        in_specs=[pl.BlockSpec((rows, width), lambda j, r: (r, j))],
        out_specs=pl.BlockSpec((1, rows, width), lambda j, r: (j, r, 0)),
        out_shape=jax.ShapeDtypeStruct((n // width, k, width), BF16),
        compiler_params=_cparams(("parallel", "parallel")),
        name="cast_col_blocks",
    )(w)


def _qk_norm_rope(z, g, cos, sin_signed):
    tm = z.shape[0]
    lane = lax.broadcasted_iota(I32, (tm, LANES), 1)
    lo = lane < D_HEAD
    first = (lane & (D_HEAD // 2)) == 0
    outs = []
    for c in range(z.shape[1] // LANES):
        zc = z[:, c * LANES:(c + 1) * LANES]
        zz = zc * zc
        s_lo = jnp.sum(jnp.where(lo, zz, 0.0), axis=-1, keepdims=True)
        s_hi = jnp.sum(jnp.where(lo, 0.0, zz), axis=-1, keepdims=True)
        r = jnp.where(lo, lax.rsqrt(s_lo * (1.0 / D_HEAD) + EPS), lax.rsqrt(s_hi * (1.0 / D_HEAD) + EPS))
        y = zc * r * g
        up = pltpu.roll(y, LANES - D_HEAD // 2, 1)
        dn = pltpu.roll(y, D_HEAD // 2, 1)
        outs.append(y * cos + jnp.where(first, up, dn) * sin_signed)
    return jnp.concatenate(outs, axis=1)


def _in_proj_kernel(x_ref, g1_ref, w_ref, qg_ref, kg_ref,
                    glu_ref, q_ref, k_ref, kb_ref, v_ref, vb_ref, gate_ref,
                    h_scr, za_scr, z0_scr, z1_scr, cos_scr, sin_scr,
                    *, seq_len, pos_offset, tm, k_transposed, n_blocks):
    i = pl.program_id(0)
    j = pl.program_id(1)
    z_scr = (z0_scr, z1_scr)

    def prologue():
        x = x_ref[...]
        ms = jnp.mean(x * x, axis=-1, keepdims=True)
        h_scr[...] = (x * lax.rsqrt(ms + EPS) * g1_ref[...]).astype(BF16)
        row = lax.broadcasted_iota(I32, (tm, LANES), 0) + i * tm
        pos = (row & (seq_len - 1)) + pos_offset
        lane = lax.broadcasted_iota(I32, (tm, LANES), 1)
        f = (lane & (D_HEAD // 2 - 1)).astype(F32)
        inv = jnp.power(jnp.float32(ROPE_THETA), -f / (D_HEAD // 2))
        ang = pos.astype(F32) * inv
        cos_scr[...] = jnp.cos(ang)
        s = jnp.sin(ang)
        sin_scr[...] = jnp.where((lane & (D_HEAD // 2)) == 0, -s, s)

    def finish(blk, z):
        if blk == 0:
            za_scr[...] = z
        elif blk == 1:
            glu_ref[...] = za_scr[...] * jax.nn.sigmoid(z)
        elif blk == 2:
            q = _qk_norm_rope(z, qg_ref[...], cos_scr[...], sin_scr[...])
            q_ref[...] = (q * (D_HEAD ** -0.5 * LOG2_E)).astype(BF16)
        elif blk == 3:
            k = _qk_norm_rope(z, kg_ref[...], cos_scr[...], sin_scr[...])
            if k_transposed:
                kt = k.T
                k_ref[0] = kt
                kb_ref[0] = kt.astype(BF16)
            else:
                k_ref[...] = k
                kb_ref[...] = k.astype(BF16)
        elif blk == 4:
            v_ref[...] = z
            vb_ref[...] = z.astype(BF16)
        else:
            gate_ref[...] = jax.nn.sigmoid(z).astype(BF16)

    for step in range(n_blocks + 1):
        @pl.when(j == step)
        def _(step=step):
            if step == 0:
                prologue()
            if step >= 1:
                finish(step - 1, z_scr[(step - 1) % 2][...])
            if step < n_blocks:
                z_scr[step % 2][...] = jnp.dot(h_scr[...], w_ref[0], preferred_element_type=F32)


def _in_proj(x2d, g1, w_in_blocks, qg, kg, *, seq_len, pos_offset, tm, k_transposed):
    n, d = x2d.shape
    nj, _, tn = w_in_blocks.shape
    assert n % tm == 0 and tn == d // 2 and nj == 9
    assert seq_len & (seq_len - 1) == 0
    blk = lambda: pl.BlockSpec((tm, tn), lambda i, j: (i, 0))
    if k_transposed:
        assert seq_len % tm == 0
        per = seq_len // tm
        k_shape = (n // seq_len, tn, seq_len)
        kblk = lambda: pl.BlockSpec((1, tn, tm), lambda i, j: (i // per, 0, i % per))
    else:
        k_shape = (n, tn)
        kblk = blk
    out_shape = (
        jax.ShapeDtypeStruct((n, tn), F32),
        jax.ShapeDtypeStruct((n, tn), BF16),
        jax.ShapeDtypeStruct(k_shape, F32),
        jax.ShapeDtypeStruct(k_shape, BF16),
        jax.ShapeDtypeStruct((n, tn), F32),
        jax.ShapeDtypeStruct((n, tn), BF16),
        jax.ShapeDtypeStruct((n, 4 * tn), BF16),
    )
    return pl.pallas_call(
        functools.partial(_in_proj_kernel, seq_len=seq_len, pos_offset=pos_offset, tm=tm,
                          k_transposed=k_transposed, n_blocks=nj),
        grid=(n // tm, nj + 1),
        in_specs=[
            pl.BlockSpec((tm, d), lambda i, j: (i, 0)),
            pl.BlockSpec((1, d), lambda i, j: (0, 0)),
            pl.BlockSpec((1, d, tn), lambda i, j: (jnp.minimum(j, nj - 1), 0, 0)),
            pl.BlockSpec((1, LANES), lambda i, j: (0, 0)),
            pl.BlockSpec((1, LANES), lambda i, j: (0, 0)),
        ],
        out_specs=(blk(), blk(), kblk(), kblk(), blk(), blk(),
                   pl.BlockSpec((tm, tn), lambda i, j: (i, jnp.maximum(j - 6, 0)))),
        out_shape=out_shape,
        scratch_shapes=[
            pltpu.VMEM((tm, d), BF16),
            pltpu.VMEM((tm, tn), F32),
            pltpu.VMEM((tm, tn), F32),
            pltpu.VMEM((tm, tn), F32),
            pltpu.VMEM((tm, LANES), F32),
            pltpu.VMEM((tm, LANES), F32),
        ],
        compiler_params=_cparams(("parallel", "arbitrary")),
        name="in_proj",
    )(x2d, g1, w_in_blocks, qg, kg)


CONV_ROWS = 64


def _conv_kernel(cur_ref, prev_ref, st_ref, w_ref, b_ref, lg_ref, lb_ref,
                 c_ref, so_ref, buf, shifted, *, tt):
    ti = pl.program_id(1)
    dc = cur_ref.shape[2]

    @pl.when(ti == 0)
    def _():
        buf[0:CONV_PAD, :] = st_ref[0]

    @pl.when(ti > 0)
    def _():
        buf[0:CONV_PAD, :] = prev_ref[0]

    buf[CONV_PAD:CONV_PAD + tt, :] = cur_ref[0]
    lead = CONV_PAD - CONV_STATE
    rows = min(CONV_ROWS, tt)
    span = shifted.shape[1]
    for p in range(1, SUBLANES):
        shifted[p - 1] = buf[p:p + span, :]

    for rs in range(tt // rows):
        parts = []
        for cs in range(dc // LANES):
            sl = slice(cs * LANES, (cs + 1) * LANES)
            acc = jnp.zeros((rows, LANES), F32)
            for j in range(CONV_WIDTH):
                phase = (lead + j) % SUBLANES
                r0 = lead + j - phase + rs * rows
                src = buf if phase == 0 else shifted.at[phase - 1]
                acc = acc + src[r0:r0 + rows, sl] * w_ref[j:j + 1, sl]
            parts.append(acc + b_ref[:, sl])
        c = jnp.concatenate(parts, axis=1)
        mu = jnp.mean(c, axis=-1, keepdims=True)
        var = jnp.mean(jnp.square(c - mu), axis=-1, keepdims=True)
        y = (c - mu) * lax.rsqrt(var + EPS) * lg_ref[...] + lb_ref[...]
        c_ref[0, rs * rows:(rs + 1) * rows, :] = (y * jax.nn.sigmoid(y)).astype(BF16)

    @pl.when(ti == pl.num_programs(1) - 1)
    def _():
        so_ref[0] = buf[lead + tt:lead + tt + CONV_STATE, :]


def _conv(glu3d, hist, w, b, lg, lb, *, tt):
    bsz, t, dc = glu3d.shape
    assert t % tt == 0 and tt % min(CONV_ROWS, tt) == 0 and tt % CONV_PAD == 0
    per = tt // CONV_PAD
    return pl.pallas_call(
        functools.partial(_conv_kernel, tt=tt),
        grid=(bsz, t // tt),
        in_specs=[
            pl.BlockSpec((1, tt, dc), lambda bi, ti: (bi, ti, 0)),
            pl.BlockSpec((1, CONV_PAD, dc), lambda bi, ti: (bi, jnp.maximum(ti * per - 1, 0), 0)),
            pl.BlockSpec((1, CONV_PAD, dc), lambda bi, ti: (bi, 0, 0)),
            pl.BlockSpec((CONV_WIDTH, dc), lambda bi, ti: (0, 0)),
            pl.BlockSpec((1, dc), lambda bi, ti: (0, 0)),
            pl.BlockSpec((1, dc), lambda bi, ti: (0, 0)),
            pl.BlockSpec((1, dc), lambda bi, ti: (0, 0)),
        ],
        out_specs=(
            pl.BlockSpec((1, tt, dc), lambda bi, ti: (bi, ti, 0)),
            pl.BlockSpec((1, CONV_STATE, dc), lambda bi, ti: (bi, 0, 0)),
        ),
        out_shape=(
            jax.ShapeDtypeStruct((bsz, t, dc), BF16),
            jax.ShapeDtypeStruct((bsz, CONV_STATE, dc), F32),
        ),
        scratch_shapes=[
            pltpu.VMEM((CONV_PAD + tt, dc), F32),
            pltpu.VMEM((SUBLANES - 1, tt + CONV_PAD - SUBLANES, dc), F32),
        ],
        compiler_params=_cparams(("parallel", "arbitrary")),
        name="conv_branch",
    )(glu3d, glu3d, hist, w, b, lg, lb)


def _lambda_value(lq1_ref, lk1_ref, lq2_ref, lk2_ref, lam_init):
    a = jnp.sum(lq1_ref[...] * lk1_ref[...], axis=-1, keepdims=True)
    b = jnp.sum(lq2_ref[...] * lk2_ref[...], axis=-1, keepdims=True)
    return jnp.exp(a) - jnp.exp(b) + lam_init


def _head_finish(o1, o2, lam, sg, lam_init):
    o = o1 - lam * o2
    ms = jnp.mean(o * o, axis=-1, keepdims=True)
    return (o * lax.rsqrt(ms + EPS) * sg) * (1.0 - lam_init)


def _split_heads(q):
    lane = lax.broadcasted_iota(I32, q.shape, 1)
    zero = jnp.zeros_like(q)
    return jnp.where(lane < D_HEAD, q, zero), jnp.where(lane >= D_HEAD, q, zero)


ATTN_FULL_FULL, ATTN_FULL_DIAG, ATTN_DIAG = 0, 1, 2


def _attn_blocks(qcs, blocks, state):
    m, acc = state
    scores = []
    for kt, _, diagonal in blocks:
        ss = [jnp.dot(qc, kt, preferred_element_type=F32) for qc in qcs]
        if diagonal:
            tq, tk = ss[0].shape
            row_chunk = lax.broadcasted_iota(I32, (tq, tk), 0) // CHUNK
            col_chunk = lax.broadcasted_iota(I32, (tq, tk), 1) // CHUNK
            mask = col_chunk <= row_chunk
            ss = [jnp.where(mask, s, NEG_INF) for s in ss]
        scores.append(ss)
    for ss, (_, v, _) in zip(scores, blocks):
        tk = ss[0].shape[1]
        v_ext = jnp.concatenate([v, jnp.ones_like(v)], axis=1)
        m_new = [jnp.maximum(m[c], jnp.max(ss[c], axis=-1, keepdims=True)) for c in range(2)]
        ps = [jnp.exp2(ss[c] - jnp.concatenate([m_new[c]] * (tk // LANES), axis=1)).astype(BF16) for c in range(2)]
        pvs = [jnp.dot(ps[c], v_ext, preferred_element_type=F32) for c in range(2)]
        alphas = [jnp.exp2(m[c] - m_new[c]) for c in range(2)]
        acc = [jnp.concatenate([alphas[c], alphas[c]], axis=1) * acc[c] + pvs[c] for c in range(2)]
        m = m_new
    return m, acc


def _attn_prompt_kernel(qi_tab, ka_tab, kb_tab, kind_tab, q_ref, kta_ref, ktb_ref, va_ref, vb_ref,
                        lq1_ref, lk1_ref, lq2_ref, lk2_ref, sg_ref, uf_ref, vf_ref,
                        o_ref, ub_ref, vtb_ref, m_scr, acc_scr, *, lam_init):
    s_id = pl.program_id(2)
    kind = kind_tab[s_id]

    def convert():
        ub_ref[...] = uf_ref[...].astype(BF16)
        vtb_ref[0] = vf_ref[...].T.astype(BF16)

    @pl.when(ka_tab[s_id] == 0)
    def _():
        m_scr[...] = jnp.full(m_scr.shape, -jnp.inf, F32)
        acc_scr[...] = jnp.zeros(acc_scr.shape, F32)

    def load():
        return _split_heads(q_ref[0]), ([m_scr[c] for c in range(2)], [acc_scr[c] for c in range(2)])

    def store(state):
        for c in range(2):
            m_scr[c] = state[0][c]
            acc_scr[c] = state[1][c]

    def finish(state):
        lam = _lambda_value(lq1_ref, lk1_ref, lq2_ref, lk2_ref, lam_init)
        a1, a2 = state[1]
        y = _head_finish(a1[:, :D_VHEAD] / a1[:, D_VHEAD:], a2[:, :D_VHEAD] / a2[:, D_VHEAD:],
                         lam, sg_ref[...], lam_init)
        o_ref[0] = y.astype(BF16)

    @pl.when(kind == ATTN_FULL_FULL)
    def _():
        convert()
        qcs, state = load()
        store(_attn_blocks(qcs, [(kta_ref[0], va_ref[0], False), (ktb_ref[0], vb_ref[0], False)], state))

    @pl.when(kind == ATTN_FULL_DIAG)
    def _():
        convert()
        qcs, state = load()
        finish(_attn_blocks(qcs, [(kta_ref[0], va_ref[0], False), (ktb_ref[0], vb_ref[0], True)], state))

    @pl.when(kind == ATTN_DIAG)
    def _():
        convert()
        qcs, state = load()
        finish(_attn_blocks(qcs, [(kta_ref[0], va_ref[0], True)], state))


def _attn_steps(nq):
    steps = []
    for qi in range(nq):
        ki = 0
        while ki + 1 < qi:
            steps.append((qi, ki, ki + 1, ATTN_FULL_FULL))
            ki += 2
        if ki + 1 == qi:
            steps.append((qi, ki, qi, ATTN_FULL_DIAG))
        else:
            steps.append((qi, qi, qi, ATTN_DIAG))
    return steps


def _pow2_floor(n):
    return 1 << (n.bit_length() - 1)


def _attn_prompt(q3, kt3, v3, lq1, lk1, lq2, lk2, sg, peer_u, peer_v, *, tq, ce, lam_init):
    bsz, t, da = q3.shape
    ne, d = peer_u.shape
    nq = t // tq
    assert t % tq == 0 and tq % CHUNK == 0 and tq % LANES == 0
    steps = _attn_steps(nq)
    ns = len(steps)
    tabs = [jnp.asarray([st[i] for st in steps], I32) for i in range(4)]
    total = bsz * N_HEADS * ns
    n_u = min(_pow2_floor(total), ne // 16)
    n_v = min(_pow2_floor(total), ne // LANES)
    ru, rv = ne // n_u, ne // n_v
    assert ne % n_u == 0 and ne % n_v == 0 and ce % rv == 0
    flat = lambda b, h, s: (b * N_HEADS + h) * ns + s
    u_idx = lambda b, h, s, *_: (jnp.minimum(flat(b, h, s), n_u - 1), 0)
    v_idx = lambda b, h, s, *_: (jnp.minimum(flat(b, h, s), n_v - 1), 0)

    def vt_idx(b, h, s, *_):
        piece = jnp.minimum(flat(b, h, s), n_v - 1)
        return (piece // (ce // rv), 0, piece % (ce // rv))

    vec = lambda n: pl.BlockSpec((1, n), lambda b, h, s, *_: (0, 0))
    grid_spec = pltpu.PrefetchScalarGridSpec(
        num_scalar_prefetch=4,
        grid=(bsz, N_HEADS, ns),
        in_specs=[
            pl.BlockSpec((1, tq, LANES), lambda b, h, s, qt, ka, kb, kd: (b, qt[s], h)),
            pl.BlockSpec((1, LANES, tq), lambda b, h, s, qt, ka, kb, kd: (b, h, ka[s])),
            pl.BlockSpec((1, LANES, tq), lambda b, h, s, qt, ka, kb, kd: (b, h, kb[s])),
            pl.BlockSpec((1, tq, LANES), lambda b, h, s, qt, ka, kb, kd: (b, ka[s], h)),
            pl.BlockSpec((1, tq, LANES), lambda b, h, s, qt, ka, kb, kd: (b, kb[s], h)),
            vec(D_HEAD), vec(D_HEAD), vec(D_HEAD), vec(D_HEAD), vec(D_VHEAD),
            pl.BlockSpec((ru, d), u_idx),
            pl.BlockSpec((rv, d), v_idx),
        ],
        out_specs=(
            pl.BlockSpec((1, tq, LANES), lambda b, h, s, qt, ka, kb, kd: (b, qt[s], h)),
            pl.BlockSpec((ru, d), u_idx),
            pl.BlockSpec((1, d, rv), vt_idx),
        ),
        scratch_shapes=[
            pltpu.VMEM((2, tq, LANES), F32),
            pltpu.VMEM((2, tq, 2 * D_VHEAD), F32),
        ],
    )
    return pl.pallas_call(
        functools.partial(_attn_prompt_kernel, lam_init=lam_init),
        grid_spec=grid_spec,
        out_shape=(
            jax.ShapeDtypeStruct((bsz, t, da), BF16),
            jax.ShapeDtypeStruct((ne, d), BF16),
            jax.ShapeDtypeStruct((ne // ce, d, ce), BF16),
        ),
        compiler_params=_cparams(("arbitrary", "arbitrary", "arbitrary")),
        name="attn_prompt",
    )(*tabs, q3, kt3, kt3, v3, v3, lq1, lk1, lq2, lk2, sg, peer_u, peer_v)


def _attn_sample_kernel(q_ref, kn_ref, vn_ref, ck_ref, cv_ref, lq1_ref, lk1_ref, lq2_ref, lk2_ref, sg_ref,
                        o_ref, *, lam_init):
    past = ck_ref.shape[2]
    lam = _lambda_value(lq1_ref, lk1_ref, lq2_ref, lk2_ref, lam_init)
    for h in range(N_HEADS):
        sl = slice(h * LANES, (h + 1) * LANES)
        q = q_ref[0, :, sl]
        kn = kn_ref[0, :, sl]
        vn = vn_ref[0, :, sl]
        kpt = ck_ref[0, sl, :].astype(BF16)
        vp = cv_ref[0, pl.ds(h, past, stride=N_HEADS), :].astype(BF16)
        outs = []
        for qc in _split_heads(q):
            sp = jnp.dot(qc, kpt, preferred_element_type=F32)
            sn = lax.dot_general(qc, kn, NT_DIMS, preferred_element_type=F32)
            m = jnp.maximum(jnp.max(sp, axis=-1, keepdims=True), jnp.max(sn, axis=-1, keepdims=True))
            pp = jnp.exp2(sp - m)
            pn = jnp.exp2(sn - m)
            l = jnp.sum(pp, axis=-1, keepdims=True) + jnp.sum(pn, axis=-1, keepdims=True)
            o = (jnp.dot(pp.astype(BF16), vp, preferred_element_type=F32)
                 + jnp.dot(pn.astype(BF16), vn, preferred_element_type=F32))
            outs.append(o / l)
        o_ref[0, :, sl] = _head_finish(outs[0], outs[1], lam, sg_ref[...], lam_init).astype(BF16)


def _attn_sample(q3, kn3, vn3, ckt3, cv3, lq1, lk1, lq2, lk2, sg, *, lam_init):
    bsz, t, da = q3.shape
    past = ckt3.shape[2]
    vec = lambda n: pl.BlockSpec((1, n), lambda b: (0, 0))
    new = lambda: pl.BlockSpec((1, t, da), lambda b: (b, 0, 0))
    return pl.pallas_call(
        functools.partial(_attn_sample_kernel, lam_init=lam_init),
        grid=(bsz,),
        in_specs=[new(), new(), new(),
                  pl.BlockSpec((1, da, past), lambda b: (b, 0, 0)),
                  pl.BlockSpec((1, past * N_HEADS, D_VHEAD), lambda b: (b, 0, 0)),
                  vec(D_HEAD), vec(D_HEAD), vec(D_HEAD), vec(D_HEAD), vec(D_VHEAD)],
        out_specs=new(),
        out_shape=jax.ShapeDtypeStruct((bsz, t, da), BF16),
        compiler_params=_cparams(("parallel",)),
        name="attn_sample",
    )(q3, kn3, vn3, ckt3, cv3, lq1, lk1, lq2, lk2, sg)


def _merge_kernel(c_ref, o_ref, gc_ref, ga_ref, wco_ref, wao_ref, wout_ref, x_ref, y_ref):
    j = pl.program_id(1)

    @pl.when(j == 0)
    def _():
        y_ref[...] = x_ref[...]

    co = jnp.dot(c_ref[...], wco_ref[0], preferred_element_type=F32)
    ao = jnp.dot(o_ref[...], wao_ref[0], preferred_element_type=F32)
    merged = gc_ref[...].astype(F32) * co + ga_ref[...].astype(F32) * ao
    y_ref[...] += jnp.dot(merged.astype(BF16), wout_ref[...], preferred_element_type=F32)


def _merge(c2d, o2d, gates, wco, wao, wout, x2d, *, tm):
    n, d = x2d.shape
    dh = d // 2
    assert n % tm == 0
    return pl.pallas_call(
        _merge_kernel,
        grid=(n // tm, 2),
        in_specs=[
            pl.BlockSpec((tm, dh), lambda i, j: (i, 0)),
            pl.BlockSpec((tm, dh), lambda i, j: (i, 0)),
            pl.BlockSpec((tm, dh), lambda i, j: (i, j)),
            pl.BlockSpec((tm, dh), lambda i, j: (i, 2 + j)),
            pl.BlockSpec((1, dh, dh), lambda i, j: (j, 0, 0)),
            pl.BlockSpec((1, dh, dh), lambda i, j: (j, 0, 0)),
            pl.BlockSpec((dh, d), lambda i, j: (j, 0)),
            pl.BlockSpec((tm, d), lambda i, j: (i, 0)),
        ],
        out_specs=pl.BlockSpec((tm, d), lambda i, j: (i, 0)),
        out_shape=jax.ShapeDtypeStruct((n, d), F32),
        compiler_params=_cparams(("parallel", "arbitrary")),
        name="merge_out_proj",
    )(c2d, o2d, gates, gates, wco, wao, wout, x2d)


def _topk_rows(scores, cur_ref, rank_ref, val_ref, k):
    rows = scores.shape[0]

    cur_ref[...] = scores
    rank_ref[...] = jnp.full(rank_ref.shape, float(k), F32)

    def fast(j, m):
        cur = cur_ref[...]
        hit = cur == m
        rank_ref[...] = jnp.where(hit, lax.convert_element_type(j, F32), rank_ref[...])
        nxt = jnp.where(hit, -jnp.inf, cur)
        cur_ref[...] = nxt
        val_ref[j] = m
        return jnp.max(nxt, axis=0, keepdims=True)

    lax.fori_loop(0, k, fast, jnp.max(scores, axis=0, keepdims=True))
    ranked = jnp.sum(jnp.where(rank_ref[...] < float(k), 1.0, 0.0), axis=0, keepdims=True)
    tied = jnp.max(ranked) > float(k)

    @pl.when(tied)
    def _():
        iota = lax.broadcasted_iota(I32, (rows, scores.shape[1]), 0).astype(F32)
        cur_ref[...] = scores
        rank_ref[...] = jnp.full(rank_ref.shape, float(k), F32)

        def exact(j, carry):
            cur = cur_ref[...]
            m = jnp.max(cur, axis=0, keepdims=True)
            idx = jnp.min(jnp.where(cur == m, iota, float(rows)), axis=0, keepdims=True)
            hit = iota == idx
            rank_ref[...] = jnp.where(hit, lax.convert_element_type(j, F32), rank_ref[...])
            cur_ref[...] = jnp.where(hit, -jnp.inf, cur)
            val_ref[j] = m
            return carry

        lax.fori_loop(0, k, exact, 0)


def _cand_layout(kk):
    counts = [kk // (j1 + 1) for j1 in range(kk)]
    starts = [sum(counts[:j1]) for j1 in range(kk)]
    return counts, starts, sum(counts)


def _peer_route(x_ref, g2_ref, wq_ref, k1_ref, k2_ref, yt_scr, ht_scr, q_scr, e2_scr, rank2_scr, c_scr, n_scr,
                cur_scr, rank_scr, rank1_scr, cand_scr, crank_scr, val1_scr, val2_scr, valc_scr, *, tm):
    nk = k1_ref.shape[0]
    kk = PEER_TOPK
    counts, starts, ncand = _cand_layout(kk)
    x = x_ref[...]
    yt_scr[...] = jnp.zeros(yt_scr.shape, F32)
    ms = jnp.mean(x * x, axis=-1, keepdims=True)
    hf = x * lax.rsqrt(ms + EPS) * g2_ref[...]
    ht_scr[...] = hf.T.astype(BF16)
    q = jnp.dot(hf.astype(BF16), wq_ref[...], preferred_element_type=F32)
    for hd in range(PEER_HEADS):
        q_scr[hd] = q[:, hd * LANES:(hd + 1) * LANES]

    def head_body(hd, carry):
        qh = q_scr[hd]
        s1 = lax.dot_general(k1_ref[...], qh, NT_DIMS, preferred_element_type=F32)
        s2 = lax.dot_general(k2_ref[...], qh, NT_DIMS, preferred_element_type=F32)
        _topk_rows(s1, cur_scr, rank1_scr, val1_scr, kk)
        _topk_rows(s2, cur_scr, rank_scr, val2_scr, kk)
        rank2_scr[hd] = rank_scr[...].astype(BF16)
        v1max = val1_scr[0]
        v2max = val2_scr[0]
        v2all = jnp.concatenate([val2_scr[j2] for j2 in range(kk)], axis=0)
        pieces = [val1_scr[j1] + v2all[:counts[j1], :] for j1 in range(kk)]
        pieces.append(jnp.full((cand_scr.shape[0] - ncand, tm), -jnp.inf, F32))
        cand = jnp.concatenate(pieces, axis=0)
        _topk_rows(cand, cand_scr, crank_scr, valc_scr, kk)
        picked = crank_scr[...] < float(kk)
        z = jnp.sum(jnp.where(picked, jnp.exp(cand - valc_scr[0]), 0.0), axis=0, keepdims=True)
        crow = lax.broadcasted_iota(I32, cand.shape, 0)
        rank1 = rank1_scr[...]
        n_by = jnp.zeros((nk, tm), F32)
        for j1 in range(kk):
            in_row = (crow >= starts[j1]) & (crow < starts[j1] + counts[j1])
            n_j1 = jnp.sum(jnp.where(picked & in_row, 1.0, 0.0), axis=0, keepdims=True)
            n_by = n_by + jnp.where(rank1 == float(j1), n_j1, 0.0)
        n_scr[hd] = n_by
        c_scr[hd] = jnp.exp(s1 - v1max) / z
        e2_scr[hd] = jnp.exp(s2 - v2max).astype(BF16)
        return carry

    lax.fori_loop(0, PEER_HEADS, head_body, 0)


def _peer_scores(u_ref, ht_scr, at_scr):
    at_scr[...] = jnp.dot(u_ref[...], ht_scr[...], preferred_element_type=F32)


PEER_KEYS_PER_DOT = 2


def _peer_gates(chunk, at_scr, wt_scr, e2_scr, rank2_scr, c_scr, n_scr, *, tm, ce, nk):
    for blk in range(ce // nk):
        r = chunk * (ce // nk) + blk
        acc = jnp.zeros((nk, tm), BF16)
        for hd in range(PEER_HEADS):
            n_row = n_scr[hd, pl.ds(r, 1), :].astype(BF16)
            c_row = c_scr[hd, pl.ds(r, 1), :].astype(BF16)
            acc = acc + jnp.where(rank2_scr[hd] < n_row, e2_scr[hd] * c_row, 0)
        a = at_scr[blk * nk:(blk + 1) * nk, :]
        gelu = 0.5 * a * (1.0 + lax.erf(a * (2.0 ** -0.5)))
        wt_scr[blk * nk:(blk + 1) * nk, :] = acc * gelu.astype(BF16)


def _peer_combine(wt_scr, vt_ref, yt_scr, *, nk):
    per = PEER_KEYS_PER_DOT * nk
    total = None
    for g in range(wt_scr.shape[0] // per):
        rows = slice(g * per, (g + 1) * per)
        d = jnp.dot(vt_ref[0, :, rows], wt_scr[rows, :], preferred_element_type=F32)
        total = d if total is None else total + d
    yt_scr[...] += total


def _peer_kernel(x_ref, g2_ref, wq_ref, k1_ref, k2_ref, u_ref, vt_ref, y_ref,
                 ht_scr, yt_scr, q_scr, at0_scr, at1_scr, wt_scr, e2_scr, rank2_scr, c_scr, n_scr,
                 cur_scr, rank_scr, rank1_scr, cand_scr, crank_scr, val1_scr, val2_scr, valc_scr,
                 *, tm, ce):
    s = pl.program_id(1)
    last = pl.num_programs(1) - 1
    nk = k1_ref.shape[0]
    even = s % 2 == 0

    def step(score_into, mix_from):
        if mix_from is not None:
            _peer_gates(s - 1, mix_from, wt_scr, e2_scr, rank2_scr, c_scr, n_scr, tm=tm, ce=ce, nk=nk)
        if score_into is not None:
            _peer_scores(u_ref, ht_scr, score_into)
        if mix_from is not None:
            _peer_combine(wt_scr, vt_ref, yt_scr, nk=nk)

    @pl.when(s == 0)
    def _():
        _peer_route(x_ref, g2_ref, wq_ref, k1_ref, k2_ref, yt_scr, ht_scr, q_scr, e2_scr, rank2_scr, c_scr, n_scr,
                    cur_scr, rank_scr, rank1_scr, cand_scr, crank_scr, val1_scr, val2_scr, valc_scr, tm=tm)
        step(at0_scr, None)

    @pl.when((s > 0) & (s < last) & even)
    def _():
        step(at0_scr, at1_scr)

    @pl.when((s < last) & jnp.logical_not(even))
    def _():
        step(at1_scr, at0_scr)

    @pl.when(s == last)
    def _():
        step(None, at1_scr)
        y_ref[...] = x_ref[...] + yt_scr[...].T


def _peer(x2d, g2, wq, k1p, k2p, u_bf, vt_bf, *, tm, ce):
    n, d = x2d.shape
    ne = u_bf.shape[0]
    nk = k1p.shape[0]
    kk = PEER_TOPK
    nc = ne // ce
    ncand = -(-_cand_layout(kk)[2] // 8) * 8
    assert n % tm == 0 and ne % ce == 0 and ce % (2 * nk) == 0 and ne == nk * nk and nk == LANES
    assert nc % 2 == 0
    return pl.pallas_call(
        functools.partial(_peer_kernel, tm=tm, ce=ce),
        grid=(n // tm, nc + 1),
        in_specs=[
            pl.BlockSpec((tm, d), lambda i, s: (i, 0), pipeline_mode=pl.Buffered(1)),
            pl.BlockSpec((1, d), lambda i, s: (0, 0)),
            pl.BlockSpec(wq.shape, lambda i, s: (0, 0), pipeline_mode=pl.Buffered(1)),
            pl.BlockSpec((nk, LANES), lambda i, s: (0, 0)),
            pl.BlockSpec((nk, LANES), lambda i, s: (0, 0)),
            pl.BlockSpec((ce, d), lambda i, s: (jnp.minimum(s, nc - 1), 0)),
            pl.BlockSpec((1, d, ce), lambda i, s: (jnp.maximum(s - 1, 0), 0, 0)),
        ],
        out_specs=pl.BlockSpec((tm, d), lambda i, s: (i, 0)),
        out_shape=jax.ShapeDtypeStruct((n, d), F32),
        scratch_shapes=[
            pltpu.VMEM((d, tm), BF16),
            pltpu.VMEM((d, tm), F32),
            pltpu.VMEM((PEER_HEADS, tm, LANES), F32),
            pltpu.VMEM((ce, tm), F32),
            pltpu.VMEM((ce, tm), F32),
            pltpu.VMEM((ce, tm), BF16),
            pltpu.VMEM((PEER_HEADS, nk, tm), BF16),
            pltpu.VMEM((PEER_HEADS, nk, tm), BF16),
            pltpu.VMEM((PEER_HEADS, nk, tm), F32),
            pltpu.VMEM((PEER_HEADS, nk, tm), F32),
            pltpu.VMEM((nk, tm), F32),
            pltpu.VMEM((nk, tm), F32),
            pltpu.VMEM((nk, tm), F32),
            pltpu.VMEM((ncand, tm), F32),
            pltpu.VMEM((ncand, tm), F32),
            pltpu.VMEM((kk, 1, tm), F32),
            pltpu.VMEM((kk, 1, tm), F32),
            pltpu.VMEM((kk, 1, tm), F32),
        ],
        compiler_params=_cparams(("parallel", "arbitrary")),
        name="peer",
    )(x2d, g2, wq, k1p, k2p, u_bf, vt_bf)


def _tile128(g):
    return jnp.tile(g.reshape(1, -1), (1, LANES // g.shape[-1]))


def _layer(x3, pos_offset, hist, cache_k, cache_v, p, lam_init, *, tm, tt, tq, tm_peer, ce):
    bsz, t, d = x3.shape
    n = bsz * t
    x2d = x3.reshape(n, d)
    prompt = cache_k is None
    glu, q, k, kb, v, vb, gates = _in_proj(x2d, p["norm1_g"], p["w_in"], p["qg"], p["kg"],
                                           seq_len=t, pos_offset=pos_offset, tm=tm, k_transposed=prompt)
    dc = glu.shape[1]
    c, conv_state = _conv(glu.reshape(bsz, t, dc), hist, p["conv_dw_w"], p["conv_dw_b"],
                          p["conv_ln_g"], p["conv_ln_b"], tt=tt)
    lam_args = (p["lambda_q1"], p["lambda_k1"], p["lambda_q2"], p["lambda_k2"], p["subln_g"])
    r3 = lambda a: a.reshape(bsz, t, -1)
    if prompt:
        o, peer_u, peer_vt = _attn_prompt(r3(q), kb, r3(vb), *lam_args, p["peer_u_f32"], p["peer_v_f32"],
                                          tq=tq, ce=ce, lam_init=lam_init)
        p = dict(p, peer_u=peer_u, peer_vt=peer_vt)
        k = jnp.transpose(k.reshape(bsz, N_HEADS, 2, D_HEAD, t), (0, 4, 1, 2, 3))
    else:
        past = cache_k.shape[1]
        ckt = jnp.transpose(cache_k, (0, 2, 3, 4, 1)).reshape(bsz, -1, past)
        o = _attn_sample(r3(q), r3(kb), r3(vb), ckt, cache_v.reshape(bsz, past * N_HEADS, D_VHEAD), *lam_args,
                         lam_init=lam_init)
        k = k.reshape(bsz, t, N_HEADS, 2, D_HEAD)
    x_mid = _merge(c.reshape(n, dc), o.reshape(n, -1), gates, p["w_conv_out"], p["w_attn_out"],
                   p["w_out"], x2d, tm=tm)
    y = _peer(x_mid, p["norm2_g"], p["peer_wq"], p["k1p"], p["k2p"], p["peer_u"], p["peer_vt"],
              tm=tm_peer, ce=ce)
    return (y.reshape(bsz, t, d), k, v.reshape(bsz, t, N_HEADS, D_VHEAD), conv_state), p


def kernel(x_prompt, x_sample, cache_attn_k, cache_attn_v, state_conv, norm1_g, w_in, conv_dw_w, conv_dw_b,
           conv_ln_g, conv_ln_b, w_conv_out, q_norm_g, k_norm_g, lambda_q1, lambda_k1, lambda_q2, lambda_k2,
           subln_g, w_attn_out, w_out, norm2_g, peer_wq, peer_k1, peer_k2, peer_u, peer_v):
    depth = w_in.shape[0]
    xp, xs = x_prompt, x_sample
    outs = [[] for _ in range(6)]
    row = lambda a: a.reshape(1, -1)

    def col_blocks(w, width=None):
        return _bf16_col_blocks(w, width or w.shape[1], CAST_ROWS)

    bf16 = lambda w: col_blocks(w)[0]
    d_model = x_prompt.shape[-1]
    ce = PEER_CHUNK
    for l in range(depth):
        half = peer_k1.shape[-1]
        p = dict(
            norm1_g=row(norm1_g[l]), w_in=col_blocks(w_in[l], d_model // 2),
            conv_dw_w=conv_dw_w[l], conv_dw_b=row(conv_dw_b[l]),
            conv_ln_g=row(conv_ln_g[l]), conv_ln_b=row(conv_ln_b[l]),
            w_conv_out=col_blocks(w_conv_out[l], d_model // 2),
            qg=_tile128(q_norm_g[l]), kg=_tile128(k_norm_g[l]),
            lambda_q1=row(lambda_q1[l]), lambda_k1=row(lambda_k1[l]),
            lambda_q2=row(lambda_q2[l]), lambda_k2=row(lambda_k2[l]),
            subln_g=row(subln_g[l]),
            w_attn_out=col_blocks(w_attn_out[l], d_model // 2), w_out=bf16(w_out[l]),
            norm2_g=row(norm2_g[l]), peer_wq=bf16(peer_wq[l]),
            k1p=jnp.pad(peer_k1[l], ((0, 0), (0, LANES - half))),
            k2p=jnp.pad(peer_k2[l], ((0, 0), (LANES - half, 0))),
            peer_u_f32=peer_u[l], peer_v_f32=peer_v[l],
        )
        li = _lambda_init(l)
        bp, tp, _ = xp.shape
        bs, ts, _ = xs.shape
        zero_hist = jnp.zeros((bp, CONV_PAD, state_conv.shape[-1]), xp.dtype)
        (xp, kp, vp, cp), p = _layer(xp, 0, zero_hist, None, None, p, li,
                                     tm=min(512, bp * tp), tt=min(256, tp), tq=min(512, tp),
                                     tm_peer=min(512, bp * tp), ce=ce)
        hist_s = jnp.pad(state_conv[l], ((0, 0), (CONV_PAD - CONV_STATE, 0), (0, 0)))
        (xs, ks, vs, cs), _ = _layer(xs, cache_attn_k.shape[2], hist_s, cache_attn_k[l], cache_attn_v[l], p, li,
                                     tm=min(256, bs * ts), tt=ts, tq=ts,
                                     tm_peer=min(256, bs * ts), ce=ce)
        for lst, val in zip(outs, (kp, vp, cp, ks, vs, cs)):
            lst.append(val)
    kp, vp, cp, ks, vs, cs = (jnp.stack(o) for o in outs)
    return (xp, xs, kp, vp, cp, ks, vs, cs)
```

```python
import functools
import math

import jax
import jax.numpy as jnp
from jax import lax
from jax.experimental import pallas as pl
from jax.experimental.pallas import tpu as pltpu

F32 = jnp.float32
BF16 = jnp.bfloat16
I32 = jnp.int32

LANES = 128
SUBLANES = 8
CHUNK = 64
EPS = 1e-6
N_HEADS = 8
D_HEAD = 64
D_VHEAD = 2 * D_HEAD
CONV_WIDTH = 31
CONV_STATE = CONV_WIDTH - 1
CONV_PAD = 32
ROPE_THETA = 10000.0
LOG2_E = math.log2(math.e)
NEG_INF = -1e30
PEER_HEADS = 8
PEER_TOPK = 16
PEER_CHUNK = 1024
CAST_ROWS = 512
VMEM_LIMIT = 60 * 1024 * 1024

NT_DIMS = (((1,), (1,)), ((), ()))
TN_DIMS = (((0,), (0,)), ((), ()))


def _cparams(sem):
    return pltpu.CompilerParams(dimension_semantics=sem, vmem_limit_bytes=VMEM_LIMIT)


def _lambda_init(layer):
    return 0.8 - 0.6 * math.exp(-0.3 * layer)


def _cast_kernel(w_ref, o_ref):
    o_ref[0] = w_ref[...].astype(BF16)


def _bf16_col_blocks(w, width, rows):
    k, n = w.shape
    assert k % rows == 0 and n % width == 0
    return pl.pallas_call(
        _cast_kernel,
        grid=(n // width, k // rows),
        in_specs=[pl.BlockSpec((rows, width), lambda j, r: (r, j))],
        out_specs=pl.BlockSpec((1, rows, width), lambda j, r: (j, r, 0)),
        out_shape=jax.ShapeDtypeStruct((n // width, k, width), BF16),
        compiler_params=_cparams(("parallel", "parallel")),
        name="cast_col_blocks",
    )(w)


def _qk_norm_rope(z, g, cos, sin_signed):
    tm = z.shape[0]
    lane = lax.broadcasted_iota(I32, (tm, LANES), 1)
    lo = lane < D_HEAD
    first = (lane & (D_HEAD // 2)) == 0
    outs = []
    for c in range(z.shape[1] // LANES):
        zc = z[:, c * LANES:(c + 1) * LANES]
        zz = zc * zc
        s_lo = jnp.sum(jnp.where(lo, zz, 0.0), axis=-1, keepdims=True)
        s_hi = jnp.sum(jnp.where(lo, 0.0, zz), axis=-1, keepdims=True)
        r = jnp.where(lo, lax.rsqrt(s_lo * (1.0 / D_HEAD) + EPS), lax.rsqrt(s_hi * (1.0 / D_HEAD) + EPS))
        y = zc * r * g
        up = pltpu.roll(y, LANES - D_HEAD // 2, 1)
        dn = pltpu.roll(y, D_HEAD // 2, 1)
        outs.append(y * cos + jnp.where(first, up, dn) * sin_signed)
    return jnp.concatenate(outs, axis=1)


def _in_proj_kernel(x_ref, g1_ref, w_ref, qg_ref, kg_ref,
                    glu_ref, q_ref, k_ref, kb_ref, v_ref, vb_ref, gate_ref,
                    h_scr, za_scr, z0_scr, z1_scr, cos_scr, sin_scr,
                    *, seq_len, pos_offset, tm, k_transposed, n_blocks):
    i = pl.program_id(0)
    j = pl.program_id(1)
    z_scr = (z0_scr, z1_scr)

    def prologue():
        x = x_ref[...]
        ms = jnp.mean(x * x, axis=-1, keepdims=True)
        h_scr[...] = (x * lax.rsqrt(ms + EPS) * g1_ref[...]).astype(BF16)
        row = lax.broadcasted_iota(I32, (tm, LANES), 0) + i * tm
        pos = (row & (seq_len - 1)) + pos_offset
        lane = lax.broadcasted_iota(I32, (tm, LANES), 1)
        f = (lane & (D_HEAD // 2 - 1)).astype(F32)
        inv = jnp.power(jnp.float32(ROPE_THETA), -f / (D_HEAD // 2))
        ang = pos.astype(F32) * inv
        cos_scr[...] = jnp.cos(ang)
        s = jnp.sin(ang)
        sin_scr[...] = jnp.where((lane & (D_HEAD // 2)) == 0, -s, s)

    def finish(blk, z):
        if blk == 0:
            za_scr[...] = z
        elif blk == 1:
            glu_ref[...] = za_scr[...] * jax.nn.sigmoid(z)
        elif blk == 2:
            q = _qk_norm_rope(z, qg_ref[...], cos_scr[...], sin_scr[...])
            q_ref[...] = (q * (D_HEAD ** -0.5 * LOG2_E)).astype(BF16)
        elif blk == 3:
            k = _qk_norm_rope(z, kg_ref[...], cos_scr[...], sin_scr[...])
            if k_transposed:
                kt = k.T
                k_ref[0] = kt
                kb_ref[0] = kt.astype(BF16)
            else:
                k_ref[...] = k
                kb_ref[...] = k.astype(BF16)
        elif blk == 4:
            v_ref[...] = z
            vb_ref[...] = z.astype(BF16)
        else:
            gate_ref[...] = jax.nn.sigmoid(z).astype(BF16)

    for step in range(n_blocks + 1):
        @pl.when(j == step)
        def _(step=step):
            if step == 0:
                prologue()
            if step >= 1:
                finish(step - 1, z_scr[(step - 1) % 2][...])
            if step < n_blocks:
                z_scr[step % 2][...] = jnp.dot(h_scr[...], w_ref[0], preferred_element_type=F32)


def _in_proj(x2d, g1, w_in_blocks, qg, kg, *, seq_len, pos_offset, tm, k_transposed):
    n, d = x2d.shape
    nj, _, tn = w_in_blocks.shape
    assert n % tm == 0 and tn == d // 2 and nj == 9
    assert seq_len & (seq_len - 1) == 0
    blk = lambda: pl.BlockSpec((tm, tn), lambda i, j: (i, 0))
    if k_transposed:
        assert seq_len % tm == 0
        per = seq_len // tm
        k_shape = (n // seq_len, tn, seq_len)
        kblk = lambda: pl.BlockSpec((1, tn, tm), lambda i, j: (i // per, 0, i % per))
    else:
        k_shape = (n, tn)
        kblk = blk
    out_shape = (
        jax.ShapeDtypeStruct((n, tn), F32),
        jax.ShapeDtypeStruct((n, tn), BF16),
        jax.ShapeDtypeStruct(k_shape, F32),
        jax.ShapeDtypeStruct(k_shape, BF16),
        jax.ShapeDtypeStruct((n, tn), F32),
        jax.ShapeDtypeStruct((n, tn), BF16),
        jax.ShapeDtypeStruct((n, 4 * tn), BF16),
    )
    return pl.pallas_call(
        functools.partial(_in_proj_kernel, seq_len=seq_len, pos_offset=pos_offset, tm=tm,
                          k_transposed=k_transposed, n_blocks=nj),
        grid=(n // tm, nj + 1),
        in_specs=[
            pl.BlockSpec((tm, d), lambda i, j: (i, 0)),
            pl.BlockSpec((1, d), lambda i, j: (0, 0)),
            pl.BlockSpec((1, d, tn), lambda i, j: (jnp.minimum(j, nj - 1), 0, 0)),
            pl.BlockSpec((1, LANES), lambda i, j: (0, 0)),
            pl.BlockSpec((1, LANES), lambda i, j: (0, 0)),
        ],
        out_specs=(blk(), blk(), kblk(), kblk(), blk(), blk(),
                   pl.BlockSpec((tm, tn), lambda i, j: (i, jnp.maximum(j - 6, 0)))),
        out_shape=out_shape,
        scratch_shapes=[
            pltpu.VMEM((tm, d), BF16),
            pltpu.VMEM((tm, tn), F32),
            pltpu.VMEM((tm, tn), F32),
            pltpu.VMEM((tm, tn), F32),
            pltpu.VMEM((tm, LANES), F32),
            pltpu.VMEM((tm, LANES), F32),
        ],
        compiler_params=_cparams(("parallel", "arbitrary")),
        name="in_proj",
    )(x2d, g1, w_in_blocks, qg, kg)


CONV_ROWS = 64


def _conv_kernel(cur_ref, prev_ref, st_ref, w_ref, b_ref, lg_ref, lb_ref,
                 c_ref, so_ref, buf, shifted, *, tt):
    ti = pl.program_id(1)
    dc = cur_ref.shape[2]

    @pl.when(ti == 0)
    def _():
        buf[0:CONV_PAD, :] = st_ref[0]

    @pl.when(ti > 0)
    def _():
        buf[0:CONV_PAD, :] = prev_ref[0]

    buf[CONV_PAD:CONV_PAD + tt, :] = cur_ref[0]
    lead = CONV_PAD - CONV_STATE
    rows = min(CONV_ROWS, tt)
    span = shifted.shape[1]
    for p in range(1, SUBLANES):
        shifted[p - 1] = buf[p:p + span, :]

    for rs in range(tt // rows):
        parts = []
        for cs in range(dc // LANES):
            sl = slice(cs * LANES, (cs + 1) * LANES)
            acc = jnp.zeros((rows, LANES), F32)
            for j in range(CONV_WIDTH):
                phase = (lead + j) % SUBLANES
                r0 = lead + j - phase + rs * rows
                src = buf if phase == 0 else shifted.at[phase - 1]
                acc = acc + src[r0:r0 + rows, sl] * w_ref[j:j + 1, sl]
            parts.append(acc + b_ref[:, sl])
        c = jnp.concatenate(parts, axis=1)
        mu = jnp.mean(c, axis=-1, keepdims=True)
        var = jnp.mean(jnp.square(c - mu), axis=-1, keepdims=True)
        y = (c - mu) * lax.rsqrt(var + EPS) * lg_ref[...] + lb_ref[...]
        c_ref[0, rs * rows:(rs + 1) * rows, :] = (y * jax.nn.sigmoid(y)).astype(BF16)

    @pl.when(ti == pl.num_programs(1) - 1)
    def _():
        so_ref[0] = buf[lead + tt:lead + tt + CONV_STATE, :]


def _conv(glu3d, hist, w, b, lg, lb, *, tt):
    bsz, t, dc = glu3d.shape
    assert t % tt == 0 and tt % min(CONV_ROWS, tt) == 0 and tt % CONV_PAD == 0
    per = tt // CONV_PAD
    return pl.pallas_call(
        functools.partial(_conv_kernel, tt=tt),
        grid=(bsz, t // tt),
        in_specs=[
            pl.BlockSpec((1, tt, dc), lambda bi, ti: (bi, ti, 0)),
            pl.BlockSpec((1, CONV_PAD, dc), lambda bi, ti: (bi, jnp.maximum(ti * per - 1, 0), 0)),
            pl.BlockSpec((1, CONV_PAD, dc), lambda bi, ti: (bi, 0, 0)),
            pl.BlockSpec((CONV_WIDTH, dc), lambda bi, ti: (0, 0)),
            pl.BlockSpec((1, dc), lambda bi, ti: (0, 0)),
            pl.BlockSpec((1, dc), lambda bi, ti: (0, 0)),
            pl.BlockSpec((1, dc), lambda bi, ti: (0, 0)),
        ],
        out_specs=(
            pl.BlockSpec((1, tt, dc), lambda bi, ti: (bi, ti, 0)),
            pl.BlockSpec((1, CONV_STATE, dc), lambda bi, ti: (bi, 0, 0)),
        ),
        out_shape=(
            jax.ShapeDtypeStruct((bsz, t, dc), BF16),
            jax.ShapeDtypeStruct((bsz, CONV_STATE, dc), F32),
        ),
        scratch_shapes=[
            pltpu.VMEM((CONV_PAD + tt, dc), F32),
            pltpu.VMEM((SUBLANES - 1, tt + CONV_PAD - SUBLANES, dc), F32),
        ],
        compiler_params=_cparams(("parallel", "arbitrary")),
        name="conv_branch",
    )(glu3d, glu3d, hist, w, b, lg, lb)


def _lambda_value(lq1_ref, lk1_ref, lq2_ref, lk2_ref, lam_init):
    a = jnp.sum(lq1_ref[...] * lk1_ref[...], axis=-1, keepdims=True)
    b = jnp.sum(lq2_ref[...] * lk2_ref[...], axis=-1, keepdims=True)
    return jnp.exp(a) - jnp.exp(b) + lam_init


def _head_finish(o1, o2, lam, sg, lam_init):
    o = o1 - lam * o2
    ms = jnp.mean(o * o, axis=-1, keepdims=True)
    return (o * lax.rsqrt(ms + EPS) * sg) * (1.0 - lam_init)


def _split_heads(q):
    lane = lax.broadcasted_iota(I32, q.shape, 1)
    zero = jnp.zeros_like(q)
    return jnp.where(lane < D_HEAD, q, zero), jnp.where(lane >= D_HEAD, q, zero)


ATTN_FULL_FULL, ATTN_FULL_DIAG, ATTN_DIAG = 0, 1, 2


def _attn_blocks(qcs, blocks, state):
    m, acc = state
    tq = qcs[0].shape[0]
    q2 = jnp.concatenate(qcs, axis=0)
    scores = []
    for kt, _, diagonal in blocks:
        s = jnp.dot(q2, kt, preferred_element_type=F32)
        if diagonal:
            tk = s.shape[1]
            row_chunk = (lax.broadcasted_iota(I32, (2 * tq, tk), 0) % tq) // CHUNK
            col_chunk = lax.broadcasted_iota(I32, (2 * tq, tk), 1) // CHUNK
            s = jnp.where(col_chunk <= row_chunk, s, NEG_INF)
        scores.append(s)
    for s, (_, v, _) in zip(scores, blocks):
        tk = s.shape[1]
        v_ext = jnp.concatenate([v, jnp.ones_like(v)], axis=1)
        m_new = jnp.maximum(m, jnp.max(s, axis=-1, keepdims=True))
        p = jnp.exp2(s - jnp.concatenate([m_new] * (tk // LANES), axis=1)).astype(BF16)
        alpha = jnp.exp2(m - m_new)
        acc = jnp.concatenate([alpha, alpha], axis=1) * acc + jnp.dot(p, v_ext, preferred_element_type=F32)
        m = m_new
    return m, acc


def _attn_prompt_kernel(qi_tab, ka_tab, kb_tab, kind_tab, q_ref, kta_ref, ktb_ref, va_ref, vb_ref,
                        lq1_ref, lk1_ref, lq2_ref, lk2_ref, sg_ref, uf_ref, vf_ref,
                        o_ref, ub_ref, vtb_ref, m_scr, acc_scr, *, lam_init):
    s_id = pl.program_id(2)
    kind = kind_tab[s_id]

    def convert():
        ub_ref[...] = uf_ref[...].astype(BF16)
        vtb_ref[0] = vf_ref[...].T.astype(BF16)

    @pl.when(ka_tab[s_id] == 0)
    def _():
        m_scr[...] = jnp.full(m_scr.shape, -jnp.inf, F32)
        acc_scr[...] = jnp.zeros(acc_scr.shape, F32)

    def load():
        return _split_heads(q_ref[0]), (m_scr[...], acc_scr[...])

    def store(state):
        m_scr[...] = state[0]
        acc_scr[...] = state[1]

    def finish(state):
        lam = _lambda_value(lq1_ref, lk1_ref, lq2_ref, lk2_ref, lam_init)
        tq = q_ref.shape[1]
        a1, a2 = state[1][:tq], state[1][tq:]
        y = _head_finish(a1[:, :D_VHEAD] / a1[:, D_VHEAD:], a2[:, :D_VHEAD] / a2[:, D_VHEAD:],
                         lam, sg_ref[...], lam_init)
        o_ref[0] = y.astype(BF16)

    @pl.when(kind == ATTN_FULL_FULL)
    def _():
        convert()
        qcs, state = load()
        store(_attn_blocks(qcs, [(kta_ref[0], va_ref[0], False), (ktb_ref[0], vb_ref[0], False)], state))

    @pl.when(kind == ATTN_FULL_DIAG)
    def _():
        convert()
        qcs, state = load()
        finish(_attn_blocks(qcs, [(kta_ref[0], va_ref[0], False), (ktb_ref[0], vb_ref[0], True)], state))

    @pl.when(kind == ATTN_DIAG)
    def _():
        convert()
        qcs, state = load()
        finish(_attn_blocks(qcs, [(kta_ref[0], va_ref[0], True)], state))


def _attn_steps(nq):
    steps = []
    for qi in range(nq):
        ki = 0
        while ki + 1 < qi:
            steps.append((qi, ki, ki + 1, ATTN_FULL_FULL))
            ki += 2
        if ki + 1 == qi:
            steps.append((qi, ki, qi, ATTN_FULL_DIAG))
        else:
            steps.append((qi, qi, qi, ATTN_DIAG))
    return steps


def _pow2_floor(n):
    return 1 << (n.bit_length() - 1)


def _attn_prompt(q3, kt3, v3, lq1, lk1, lq2, lk2, sg, peer_u, peer_v, *, tq, ce, lam_init):
    bsz, t, da = q3.shape
    ne, d = peer_u.shape
    nq = t // tq
    assert t % tq == 0 and tq % CHUNK == 0 and tq % LANES == 0
    steps = _attn_steps(nq)
    ns = len(steps)
    tabs = [jnp.asarray([st[i] for st in steps], I32) for i in range(4)]
    total = bsz * N_HEADS * ns
    n_u = min(_pow2_floor(total), ne // 16)
    n_v = min(_pow2_floor(total), ne // LANES)
    ru, rv = ne // n_u, ne // n_v
    assert ne % n_u == 0 and ne % n_v == 0 and ce % rv == 0
    flat = lambda b, h, s: (b * N_HEADS + h) * ns + s
    u_idx = lambda b, h, s, *_: (jnp.minimum(flat(b, h, s), n_u - 1), 0)
    v_idx = lambda b, h, s, *_: (jnp.minimum(flat(b, h, s), n_v - 1), 0)

    def vt_idx(b, h, s, *_):
        piece = jnp.minimum(flat(b, h, s), n_v - 1)
        return (piece // (ce // rv), 0, piece % (ce // rv))

    vec = lambda n: pl.BlockSpec((1, n), lambda b, h, s, *_: (0, 0))
    grid_spec = pltpu.PrefetchScalarGridSpec(
        num_scalar_prefetch=4,
        grid=(bsz, N_HEADS, ns),
        in_specs=[
            pl.BlockSpec((1, tq, LANES), lambda b, h, s, qt, ka, kb, kd: (b, qt[s], h)),
            pl.BlockSpec((1, LANES, tq), lambda b, h, s, qt, ka, kb, kd: (b, h, ka[s])),
            pl.BlockSpec((1, LANES, tq), lambda b, h, s, qt, ka, kb, kd: (b, h, kb[s])),
            pl.BlockSpec((1, tq, LANES), lambda b, h, s, qt, ka, kb, kd: (b, ka[s], h)),
            pl.BlockSpec((1, tq, LANES), lambda b, h, s, qt, ka, kb, kd: (b, kb[s], h)),
            vec(D_HEAD), vec(D_HEAD), vec(D_HEAD), vec(D_HEAD), vec(D_VHEAD),
            pl.BlockSpec((ru, d), u_idx),
            pl.BlockSpec((rv, d), v_idx),
        ],
        out_specs=(
            pl.BlockSpec((1, tq, LANES), lambda b, h, s, qt, ka, kb, kd: (b, qt[s], h)),
            pl.BlockSpec((ru, d), u_idx),
            pl.BlockSpec((1, d, rv), vt_idx),
        ),
        scratch_shapes=[
            pltpu.VMEM((2 * tq, LANES), F32),
            pltpu.VMEM((2 * tq, 2 * D_VHEAD), F32),
        ],
    )
    return pl.pallas_call(
        functools.partial(_attn_prompt_kernel, lam_init=lam_init),
        grid_spec=grid_spec,
        out_shape=(
            jax.ShapeDtypeStruct((bsz, t, da), BF16),
            jax.ShapeDtypeStruct((ne, d), BF16),
            jax.ShapeDtypeStruct((ne // ce, d, ce), BF16),
        ),
        compiler_params=_cparams(("arbitrary", "arbitrary", "arbitrary")),
        name="attn_prompt",
    )(*tabs, q3, kt3, kt3, v3, v3, lq1, lk1, lq2, lk2, sg, peer_u, peer_v)


def _attn_sample_kernel(q_ref, kn_ref, vn_ref, ck_ref, cv_ref, lq1_ref, lk1_ref, lq2_ref, lk2_ref, sg_ref,
                        o_ref, *, lam_init):
    past = ck_ref.shape[2]
    lam = _lambda_value(lq1_ref, lk1_ref, lq2_ref, lk2_ref, lam_init)
    for h in range(N_HEADS):
        sl = slice(h * LANES, (h + 1) * LANES)
        q = q_ref[0, :, sl]
        kn = kn_ref[0, :, sl]
        vn = vn_ref[0, :, sl]
        kpt = ck_ref[0, sl, :].astype(BF16)
        vp = cv_ref[0, pl.ds(h, past, stride=N_HEADS), :].astype(BF16)
        outs = []
        for qc in _split_heads(q):
            sp = jnp.dot(qc, kpt, preferred_element_type=F32)
            sn = lax.dot_general(qc, kn, NT_DIMS, preferred_element_type=F32)
            m = jnp.maximum(jnp.max(sp, axis=-1, keepdims=True), jnp.max(sn, axis=-1, keepdims=True))
            pp = jnp.exp2(sp - m)
            pn = jnp.exp2(sn - m)
            l = jnp.sum(pp, axis=-1, keepdims=True) + jnp.sum(pn, axis=-1, keepdims=True)
            o = (jnp.dot(pp.astype(BF16), vp, preferred_element_type=F32)
                 + jnp.dot(pn.astype(BF16), vn, preferred_element_type=F32))
            outs.append(o / l)
        o_ref[0, :, sl] = _head_finish(outs[0], outs[1], lam, sg_ref[...], lam_init).astype(BF16)


def _attn_sample(q3, kn3, vn3, ckt3, cv3, lq1, lk1, lq2, lk2, sg, *, lam_init):
    bsz, t, da = q3.shape
    past = ckt3.shape[2]
    vec = lambda n: pl.BlockSpec((1, n), lambda b: (0, 0))
    new = lambda: pl.BlockSpec((1, t, da), lambda b: (b, 0, 0))
    return pl.pallas_call(
        functools.partial(_attn_sample_kernel, lam_init=lam_init),
        grid=(bsz,),
        in_specs=[new(), new(), new(),
                  pl.BlockSpec((1, da, past), lambda b: (b, 0, 0)),
                  pl.BlockSpec((1, past * N_HEADS, D_VHEAD), lambda b: (b, 0, 0)),
                  vec(D_HEAD), vec(D_HEAD), vec(D_HEAD), vec(D_HEAD), vec(D_VHEAD)],
        out_specs=new(),
        out_shape=jax.ShapeDtypeStruct((bsz, t, da), BF16),
        compiler_params=_cparams(("parallel",)),
        name="attn_sample",
    )(q3, kn3, vn3, ckt3, cv3, lq1, lk1, lq2, lk2, sg)


def _merge_kernel(c_ref, o_ref, gc_ref, ga_ref, wco_ref, wao_ref, wout_ref, x_ref, y_ref):
    j = pl.program_id(1)

    @pl.when(j == 0)
    def _():
        y_ref[...] = x_ref[...]

    co = jnp.dot(c_ref[...], wco_ref[0], preferred_element_type=F32)
    ao = jnp.dot(o_ref[...], wao_ref[0], preferred_element_type=F32)
    merged = gc_ref[...].astype(F32) * co + ga_ref[...].astype(F32) * ao
    y_ref[...] += jnp.dot(merged.astype(BF16), wout_ref[...], preferred_element_type=F32)


def _merge(c2d, o2d, gates, wco, wao, wout, x2d, *, tm):
    n, d = x2d.shape
    dh = d // 2
    assert n % tm == 0
    return pl.pallas_call(
        _merge_kernel,
        grid=(n // tm, 2),
        in_specs=[
            pl.BlockSpec((tm, dh), lambda i, j: (i, 0)),
            pl.BlockSpec((tm, dh), lambda i, j: (i, 0)),
            pl.BlockSpec((tm, dh), lambda i, j: (i, j)),
            pl.BlockSpec((tm, dh), lambda i, j: (i, 2 + j)),
            pl.BlockSpec((1, dh, dh), lambda i, j: (j, 0, 0)),
            pl.BlockSpec((1, dh, dh), lambda i, j: (j, 0, 0)),
            pl.BlockSpec((dh, d), lambda i, j: (j, 0)),
            pl.BlockSpec((tm, d), lambda i, j: (i, 0)),
        ],
        out_specs=pl.BlockSpec((tm, d), lambda i, j: (i, 0)),
        out_shape=jax.ShapeDtypeStruct((n, d), F32),
        compiler_params=_cparams(("parallel", "arbitrary")),
        name="merge_out_proj",
    )(c2d, o2d, gates, gates, wco, wao, wout, x2d)


def _topk_rows(scores, cur_ref, rank_ref, val_ref, k):
    rows = scores.shape[0]

    cur_ref[...] = scores
    rank_ref[...] = jnp.full(rank_ref.shape, float(k), F32)

    def fast(j, m):
        cur = cur_ref[...]
        hit = cur == m
        rank_ref[...] = jnp.where(hit, lax.convert_element_type(j, F32), rank_ref[...])
        nxt = jnp.where(hit, -jnp.inf, cur)
        cur_ref[...] = nxt
        val_ref[j] = m
        return jnp.max(nxt, axis=0, keepdims=True)

    lax.fori_loop(0, k, fast, jnp.max(scores, axis=0, keepdims=True))
    ranked = jnp.sum(jnp.where(rank_ref[...] < float(k), 1.0, 0.0), axis=0, keepdims=True)
    tied = jnp.max(ranked) > float(k)

    @pl.when(tied)
    def _():
        _topk_rows_tie_aware(scores, cur_ref, rank_ref, val_ref, k)


def _topk_rows_tie_aware(scores, cur_ref, rank_ref, val_ref, k):
    rows = scores.shape[0]
    iota = lax.broadcasted_iota(I32, scores.shape, 0).astype(F32)
    cur_ref[...] = scores
    rank_ref[...] = jnp.full(rank_ref.shape, float(k), F32)

    def exact(j, carry):
        cur = cur_ref[...]
        m = jnp.max(cur, axis=0, keepdims=True)
        idx = jnp.min(jnp.where(cur == m, iota, float(rows)), axis=0, keepdims=True)
        hit = iota == idx
        rank_ref[...] = jnp.where(hit, lax.convert_element_type(j, F32), rank_ref[...])
        cur_ref[...] = jnp.where(hit, -jnp.inf, cur)
        val_ref[j] = m
        return carry

    lax.fori_loop(0, k, exact, 0)


def _cand_layout(kk):
    counts = [kk // (j1 + 1) for j1 in range(kk)]
    starts = [sum(counts[:j1]) for j1 in range(kk)]
    return counts, starts, sum(counts)


def _peer_route(x_ref, g2_ref, wq_ref, k1_ref, k2_ref, yt_scr, ht_scr, q_scr, e2_scr, rank2_scr, c_scr, n_scr,
                cur_scr, rank_scr, rank1_scr, cand_scr, crank_scr, val1_scr, val2_scr, valc_scr, *, tm):
    nk = k1_ref.shape[0]
    kk = PEER_TOPK
    counts, starts, ncand = _cand_layout(kk)
    x = x_ref[...]
    yt_scr[...] = jnp.zeros(yt_scr.shape, F32)
    ms = jnp.mean(x * x, axis=-1, keepdims=True)
    hf = x * lax.rsqrt(ms + EPS) * g2_ref[...]
    ht_scr[...] = hf.T.astype(BF16)
    q = jnp.dot(hf.astype(BF16), wq_ref[...], preferred_element_type=F32)
    for hd in range(PEER_HEADS):
        q_scr[hd] = q[:, hd * LANES:(hd + 1) * LANES]

    def head_body(hd, carry):
        qh = q_scr[hd]
        s1 = lax.dot_general(k1_ref[...], qh, NT_DIMS, preferred_element_type=F32)
        s2 = lax.dot_general(k2_ref[...], qh, NT_DIMS, preferred_element_type=F32)
        _topk_rows(s1, cur_scr, rank1_scr, val1_scr, kk)
        _topk_rows(s2, cur_scr, rank_scr, val2_scr, kk)
        rank2_scr[hd] = rank_scr[...].astype(BF16)
        v1max = val1_scr[0]
        v2max = val2_scr[0]
        v2all = jnp.concatenate([val2_scr[j2] for j2 in range(kk)], axis=0)
        pieces = [val1_scr[j1] + v2all[:counts[j1], :] for j1 in range(kk)]
        pieces.append(jnp.full((cand_scr.shape[0] - ncand, tm), -jnp.inf, F32))
        cand = jnp.concatenate(pieces, axis=0)
        _topk_rows(cand, cand_scr, crank_scr, valc_scr, kk)
        picked = crank_scr[...] < float(kk)
        z = jnp.sum(jnp.where(picked, jnp.exp(cand - valc_scr[0]), 0.0), axis=0, keepdims=True)
        crow = lax.broadcasted_iota(I32, cand.shape, 0)
        rank1 = rank1_scr[...]
        n_by = jnp.zeros((nk, tm), F32)
        for j1 in range(kk):
            in_row = (crow >= starts[j1]) & (crow < starts[j1] + counts[j1])
            n_j1 = jnp.sum(jnp.where(picked & in_row, 1.0, 0.0), axis=0, keepdims=True)
            n_by = n_by + jnp.where(rank1 == float(j1), n_j1, 0.0)
        n_scr[hd] = n_by
        c_scr[hd] = jnp.exp(s1 - v1max) / z
        e2_scr[hd] = jnp.exp(s2 - v2max).astype(BF16)
        return carry

    lax.fori_loop(0, PEER_HEADS, head_body, 0)


def _peer_scores(u_ref, ht_scr, at_scr):
    at_scr[...] = jnp.dot(u_ref[...], ht_scr[...], preferred_element_type=F32)


PEER_KEYS_PER_DOT = 2


def _peer_gates(chunk, at_scr, wt_scr, e2_scr, rank2_scr, c_scr, n_scr, *, tm, ce, nk):
    for blk in range(ce // nk):
        r = chunk * (ce // nk) + blk
        acc = jnp.zeros((nk, tm), BF16)
        for hd in range(PEER_HEADS):
            n_row = n_scr[hd, pl.ds(r, 1), :].astype(BF16)
            c_row = c_scr[hd, pl.ds(r, 1), :].astype(BF16)
            acc = acc + jnp.where(rank2_scr[hd] < n_row, e2_scr[hd] * c_row, 0)
        a = at_scr[blk * nk:(blk + 1) * nk, :]
        gelu = 0.5 * a * (1.0 + lax.erf(a * (2.0 ** -0.5)))
        wt_scr[blk * nk:(blk + 1) * nk, :] = acc * gelu.astype(BF16)


def _peer_combine(wt_scr, vt_ref, yt_scr, *, nk):
    per = PEER_KEYS_PER_DOT * nk
    total = None
    for g in range(wt_scr.shape[0] // per):
        rows = slice(g * per, (g + 1) * per)
        d = jnp.dot(vt_ref[0, :, rows], wt_scr[rows, :], preferred_element_type=F32)
        total = d if total is None else total + d
    yt_scr[...] += total


def _peer_kernel(x_ref, g2_ref, wq_ref, k1_ref, k2_ref, u_ref, vt_ref, y_ref,
                 ht_scr, yt_scr, q_scr, at0_scr, at1_scr, wt_scr, e2_scr, rank2_scr, c_scr, n_scr,
                 cur_scr, rank_scr, rank1_scr, cand_scr, crank_scr, val1_scr, val2_scr, valc_scr,
                 *, tm, ce):
    s = pl.program_id(1)
    last = pl.num_programs(1) - 1
    nk = k1_ref.shape[0]
    even = s % 2 == 0

    def step(score_into, mix_from):
        if mix_from is not None:
            _peer_gates(s - 1, mix_from, wt_scr, e2_scr, rank2_scr, c_scr, n_scr, tm=tm, ce=ce, nk=nk)
        if score_into is not None:
            _peer_scores(u_ref, ht_scr, score_into)
        if mix_from is not None:
            _peer_combine(wt_scr, vt_ref, yt_scr, nk=nk)

    @pl.when(s == 0)
    def _():
        _peer_route(x_ref, g2_ref, wq_ref, k1_ref, k2_ref, yt_scr, ht_scr, q_scr, e2_scr, rank2_scr, c_scr, n_scr,
                    cur_scr, rank_scr, rank1_scr, cand_scr, crank_scr, val1_scr, val2_scr, valc_scr, tm=tm)
        step(at0_scr, None)

    @pl.when((s > 0) & (s < last) & even)
    def _():
        step(at0_scr, at1_scr)

    @pl.when((s < last) & jnp.logical_not(even))
    def _():
        step(at1_scr, at0_scr)

    @pl.when(s == last)
    def _():
        step(None, at1_scr)
        y_ref[...] = x_ref[...] + yt_scr[...].T


def _peer(x2d, g2, wq, k1p, k2p, u_bf, vt_bf, *, tm, ce):
    n, d = x2d.shape
    ne = u_bf.shape[0]
    nk = k1p.shape[0]
    kk = PEER_TOPK
    nc = ne // ce
    ncand = -(-_cand_layout(kk)[2] // 8) * 8
    assert n % tm == 0 and ne % ce == 0 and ce % (2 * nk) == 0 and ne == nk * nk and nk == LANES
    assert nc % 2 == 0
    return pl.pallas_call(
        functools.partial(_peer_kernel, tm=tm, ce=ce),
        grid=(n // tm, nc + 1),
        in_specs=[
            pl.BlockSpec((tm, d), lambda i, s: (i, 0), pipeline_mode=pl.Buffered(1)),
            pl.BlockSpec((1, d), lambda i, s: (0, 0)),
            pl.BlockSpec(wq.shape, lambda i, s: (0, 0), pipeline_mode=pl.Buffered(1)),
            pl.BlockSpec((nk, LANES), lambda i, s: (0, 0)),
            pl.BlockSpec((nk, LANES), lambda i, s: (0, 0)),
            pl.BlockSpec((ce, d), lambda i, s: (jnp.minimum(s, nc - 1), 0)),
            pl.BlockSpec((1, d, ce), lambda i, s: (jnp.maximum(s - 1, 0), 0, 0)),
        ],
        out_specs=pl.BlockSpec((tm, d), lambda i, s: (i, 0)),
        out_shape=jax.ShapeDtypeStruct((n, d), F32),
        scratch_shapes=[
            pltpu.VMEM((d, tm), BF16),
            pltpu.VMEM((d, tm), F32),
            pltpu.VMEM((PEER_HEADS, tm, LANES), F32),
            pltpu.VMEM((ce, tm), F32),
            pltpu.VMEM((ce, tm), F32),
            pltpu.VMEM((ce, tm), BF16),
            pltpu.VMEM((PEER_HEADS, nk, tm), BF16),
            pltpu.VMEM((PEER_HEADS, nk, tm), BF16),
            pltpu.VMEM((PEER_HEADS, nk, tm), F32),
            pltpu.VMEM((PEER_HEADS, nk, tm), F32),
            pltpu.VMEM((nk, tm), F32),
            pltpu.VMEM((nk, tm), F32),
            pltpu.VMEM((nk, tm), F32),
            pltpu.VMEM((ncand, tm), F32),
            pltpu.VMEM((ncand, tm), F32),
            pltpu.VMEM((kk, 1, tm), F32),
            pltpu.VMEM((kk, 1, tm), F32),
            pltpu.VMEM((kk, 1, tm), F32),
        ],
        compiler_params=_cparams(("parallel", "arbitrary")),
        name="peer",
    )(x2d, g2, wq, k1p, k2p, u_bf, vt_bf)


def _tile128(g):
    return jnp.tile(g.reshape(1, -1), (1, LANES // g.shape[-1]))


def _layer(x3, pos_offset, hist, cache_k, cache_v, p, lam_init, *, tm, tt, tq, tm_peer, ce):
    bsz, t, d = x3.shape
    n = bsz * t
    x2d = x3.reshape(n, d)
    prompt = cache_k is None
    glu, q, k, kb, v, vb, gates = _in_proj(x2d, p["norm1_g"], p["w_in"], p["qg"], p["kg"],
                                           seq_len=t, pos_offset=pos_offset, tm=tm, k_transposed=prompt)
    dc = glu.shape[1]
    c, conv_state = _conv(glu.reshape(bsz, t, dc), hist, p["conv_dw_w"], p["conv_dw_b"],
                          p["conv_ln_g"], p["conv_ln_b"], tt=tt)
    lam_args = (p["lambda_q1"], p["lambda_k1"], p["lambda_q2"], p["lambda_k2"], p["subln_g"])
    r3 = lambda a: a.reshape(bsz, t, -1)
    if prompt:
        o, peer_u, peer_vt = _attn_prompt(r3(q), kb, r3(vb), *lam_args, p["peer_u_f32"], p["peer_v_f32"],
                                          tq=tq, ce=ce, lam_init=lam_init)
        p = dict(p, peer_u=peer_u, peer_vt=peer_vt)
        k = jnp.transpose(k.reshape(bsz, N_HEADS, 2, D_HEAD, t), (0, 4, 1, 2, 3))
    else:
        past = cache_k.shape[1]
        ckt = jnp.transpose(cache_k, (0, 2, 3, 4, 1)).reshape(bsz, -1, past)
        o = _attn_sample(r3(q), r3(kb), r3(vb), ckt, cache_v.reshape(bsz, past * N_HEADS, D_VHEAD), *lam_args,
                         lam_init=lam_init)
        k = k.reshape(bsz, t, N_HEADS, 2, D_HEAD)
    x_mid = _merge(c.reshape(n, dc), o.reshape(n, -1), gates, p["w_conv_out"], p["w_attn_out"],
                   p["w_out"], x2d, tm=tm)
    y = _peer(x_mid, p["norm2_g"], p["peer_wq"], p["k1p"], p["k2p"], p["peer_u"], p["peer_vt"],
              tm=tm_peer, ce=ce)
    return (y.reshape(bsz, t, d), k, v.reshape(bsz, t, N_HEADS, D_VHEAD), conv_state), p


def kernel(x_prompt, x_sample, cache_attn_k, cache_attn_v, state_conv, norm1_g, w_in, conv_dw_w, conv_dw_b,
           conv_ln_g, conv_ln_b, w_conv_out, q_norm_g, k_norm_g, lambda_q1, lambda_k1, lambda_q2, lambda_k2,
           subln_g, w_attn_out, w_out, norm2_g, peer_wq, peer_k1, peer_k2, peer_u, peer_v):
    depth = w_in.shape[0]
    xp, xs = x_prompt, x_sample
    outs = [[] for _ in range(6)]
    row = lambda a: a.reshape(1, -1)

    def col_blocks(w, width=None):
        return _bf16_col_blocks(w, width or w.shape[1], CAST_ROWS)

    bf16 = lambda w: col_blocks(w)[0]
    d_model = x_prompt.shape[-1]
    ce = PEER_CHUNK
    for l in range(depth):
        half = peer_k1.shape[-1]
        p = dict(
            norm1_g=row(norm1_g[l]), w_in=col_blocks(w_in[l], d_model // 2),
            conv_dw_w=conv_dw_w[l], conv_dw_b=row(conv_dw_b[l]),
            conv_ln_g=row(conv_ln_g[l]), conv_ln_b=row(conv_ln_b[l]),
            w_conv_out=col_blocks(w_conv_out[l], d_model // 2),
            qg=_tile128(q_norm_g[l]), kg=_tile128(k_norm_g[l]),
            lambda_q1=row(lambda_q1[l]), lambda_k1=row(lambda_k1[l]),
            lambda_q2=row(lambda_q2[l]), lambda_k2=row(lambda_k2[l]),
            subln_g=row(subln_g[l]),
            w_attn_out=col_blocks(w_attn_out[l], d_model // 2), w_out=bf16(w_out[l]),
            norm2_g=row(norm2_g[l]), peer_wq=bf16(peer_wq[l]),
            k1p=jnp.pad(peer_k1[l], ((0, 0), (0, LANES - half))),
            k2p=jnp.pad(peer_k2[l], ((0, 0), (LANES - half, 0))),
            peer_u_f32=peer_u[l], peer_v_f32=peer_v[l],
        )
        li = _lambda_init(l)
        bp, tp, _ = xp.shape
        bs, ts, _ = xs.shape
        zero_hist = jnp.zeros((bp, CONV_PAD, state_conv.shape[-1]), xp.dtype)
        (xp, kp, vp, cp), p = _layer(xp, 0, zero_hist, None, None, p, li,
                                     tm=min(512, bp * tp), tt=min(256, tp), tq=min(512, tp),
                                     tm_peer=min(512, bp * tp), ce=ce)
        hist_s = jnp.pad(state_conv[l], ((0, 0), (CONV_PAD - CONV_STATE, 0), (0, 0)))
        (xs, ks, vs, cs), _ = _layer(xs, cache_attn_k.shape[2], hist_s, cache_attn_k[l], cache_attn_v[l], p, li,
                                     tm=min(256, bs * ts), tt=ts, tq=ts,
                                     tm_peer=min(256, bs * ts), ce=ce)
        for lst, val in zip(outs, (kp, vp, cp, ks, vs, cs)):
            lst.append(val)
    kp, vp, cp, ks, vs, cs = (jnp.stack(o) for o in outs)
    return (xp, xs, kp, vp, cp, ks, vs, cs)
```

```python
import functools
import math

import jax
import jax.numpy as jnp
from jax import lax
from jax.experimental import pallas as pl
from jax.experimental.pallas import tpu as pltpu

F32 = jnp.float32
BF16 = jnp.bfloat16
I32 = jnp.int32

LANES = 128
SUBLANES = 8
CHUNK = 64
EPS = 1e-6
N_HEADS = 8
D_HEAD = 64
D_VHEAD = 2 * D_HEAD
CONV_WIDTH = 31
CONV_STATE = CONV_WIDTH - 1
CONV_PAD = 32
ROPE_THETA = 10000.0
LOG2_E = math.log2(math.e)
NEG_INF = -1e30
PEER_HEADS = 8
PEER_TOPK = 16
PEER_CHUNK = 1024
CAST_ROWS = 512
VMEM_LIMIT = 60 * 1024 * 1024

NT_DIMS = (((1,), (1,)), ((), ()))
TN_DIMS = (((0,), (0,)), ((), ()))


def _cparams(sem):
    return pltpu.CompilerParams(dimension_semantics=sem, vmem_limit_bytes=VMEM_LIMIT)


def _lambda_init(layer):
    return 0.8 - 0.6 * math.exp(-0.3 * layer)


def _cast_kernel(w_ref, o_ref):
    o_ref[0] = w_ref[...].astype(BF16)


def _bf16_col_blocks(w, width, rows):
    k, n = w.shape
    assert k % rows == 0 and n % width == 0
    return pl.pallas_call(
        _cast_kernel,
        grid=(n // width, k // rows),
        in_specs=[pl.BlockSpec((rows, width), lambda j, r: (r, j))],
        out_specs=pl.BlockSpec((1, rows, width), lambda j, r: (j, r, 0)),
        out_shape=jax.ShapeDtypeStruct((n // width, k, width), BF16),
        compiler_params=_cparams(("parallel", "parallel")),
        name="cast_col_blocks",
    )(w)


def _qk_norm_rope(z, g, cos, sin_signed):
    tm = z.shape[0]
    lane = lax.broadcasted_iota(I32, (tm, LANES), 1)
    lo = lane < D_HEAD
    first = (lane & (D_HEAD // 2)) == 0
    outs = []
    for c in range(z.shape[1] // LANES):
        zc = z[:, c * LANES:(c + 1) * LANES]
        zz = zc * zc
        s_lo = jnp.sum(jnp.where(lo, zz, 0.0), axis=-1, keepdims=True)
        s_hi = jnp.sum(jnp.where(lo, 0.0, zz), axis=-1, keepdims=True)
        r = jnp.where(lo, lax.rsqrt(s_lo * (1.0 / D_HEAD) + EPS), lax.rsqrt(s_hi * (1.0 / D_HEAD) + EPS))
        y = zc * r * g
        up = pltpu.roll(y, LANES - D_HEAD // 2, 1)
        dn = pltpu.roll(y, D_HEAD // 2, 1)
        outs.append(y * cos + jnp.where(first, up, dn) * sin_signed)
    return jnp.concatenate(outs, axis=1)


def _in_proj_kernel(x_ref, g1_ref, w_ref, qg_ref, kg_ref,
                    glu_ref, q_ref, k_ref, kb_ref, v_ref, vb_ref, gate_ref,
                    h_scr, za_scr, z0_scr, z1_scr, cos_scr, sin_scr,
                    *, seq_len, pos_offset, tm, k_transposed, n_blocks):
    i = pl.program_id(0)
    j = pl.program_id(1)
    z_scr = (z0_scr, z1_scr)

    def prologue():
        x = x_ref[...]
        ms = jnp.mean(x * x, axis=-1, keepdims=True)
        h_scr[...] = (x * lax.rsqrt(ms + EPS) * g1_ref[...]).astype(BF16)
        row = lax.broadcasted_iota(I32, (tm, LANES), 0) + i * tm
        pos = (row & (seq_len - 1)) + pos_offset
        lane = lax.broadcasted_iota(I32, (tm, LANES), 1)
        f = (lane & (D_HEAD // 2 - 1)).astype(F32)
        inv = jnp.power(jnp.float32(ROPE_THETA), -f / (D_HEAD // 2))
        ang = pos.astype(F32) * inv
        cos_scr[...] = jnp.cos(ang)
        s = jnp.sin(ang)
        sin_scr[...] = jnp.where((lane & (D_HEAD // 2)) == 0, -s, s)

    def finish(blk, z):
        if blk == 0:
            za_scr[...] = z
        elif blk == 1:
            glu_ref[...] = za_scr[...] * jax.nn.sigmoid(z)
        elif blk == 2:
            q = _qk_norm_rope(z, qg_ref[...], cos_scr[...], sin_scr[...])
            q_ref[...] = (q * (D_HEAD ** -0.5 * LOG2_E)).astype(BF16)
        elif blk == 3:
            k = _qk_norm_rope(z, kg_ref[...], cos_scr[...], sin_scr[...])
            if k_transposed:
                kt = k.T
                k_ref[0] = kt
                kb_ref[0] = kt.astype(BF16)
            else:
                k_ref[...] = k
                kb_ref[...] = k.astype(BF16)
        elif blk == 4:
            v_ref[...] = z
            vb_ref[...] = z.astype(BF16)
        else:
            gate_ref[...] = jax.nn.sigmoid(z).astype(BF16)

    for step in range(n_blocks + 1):
        @pl.when(j == step)
        def _(step=step):
            if step == 0:
                prologue()
            if step >= 1:
                finish(step - 1, z_scr[(step - 1) % 2][...])
            if step < n_blocks:
                z_scr[step % 2][...] = jnp.dot(h_scr[...], w_ref[0], preferred_element_type=F32)


def _in_proj(x2d, g1, w_in_blocks, qg, kg, *, seq_len, pos_offset, tm, k_transposed):
    n, d = x2d.shape
    nj, _, tn = w_in_blocks.shape
    assert n % tm == 0 and tn == d // 2 and nj == 9
    assert seq_len & (seq_len - 1) == 0
    blk = lambda: pl.BlockSpec((tm, tn), lambda i, j: (i, 0))
    if k_transposed:
        assert seq_len % tm == 0
        per = seq_len // tm
        k_shape = (n // seq_len, tn, seq_len)
        kblk = lambda: pl.BlockSpec((1, tn, tm), lambda i, j: (i // per, 0, i % per))
    else:
        k_shape = (n, tn)
        kblk = blk
    out_shape = (
        jax.ShapeDtypeStruct((n, tn), F32),
        jax.ShapeDtypeStruct((n, tn), BF16),
        jax.ShapeDtypeStruct(k_shape, F32),
        jax.ShapeDtypeStruct(k_shape, BF16),
        jax.ShapeDtypeStruct((n, tn), F32),
        jax.ShapeDtypeStruct((n, tn), BF16),
        jax.ShapeDtypeStruct((n, 4 * tn), BF16),
    )
    return pl.pallas_call(
        functools.partial(_in_proj_kernel, seq_len=seq_len, pos_offset=pos_offset, tm=tm,
                          k_transposed=k_transposed, n_blocks=nj),
        grid=(n // tm, nj + 1),
        in_specs=[
            pl.BlockSpec((tm, d), lambda i, j: (i, 0)),
            pl.BlockSpec((1, d), lambda i, j: (0, 0)),
            pl.BlockSpec((1, d, tn), lambda i, j: (jnp.minimum(j, nj - 1), 0, 0)),
            pl.BlockSpec((1, LANES), lambda i, j: (0, 0)),
            pl.BlockSpec((1, LANES), lambda i, j: (0, 0)),
        ],
        out_specs=(blk(), blk(), kblk(), kblk(), blk(), blk(),
                   pl.BlockSpec((tm, tn), lambda i, j: (i, jnp.maximum(j - 6, 0)))),
        out_shape=out_shape,
        scratch_shapes=[
            pltpu.VMEM((tm, d), BF16),
            pltpu.VMEM((tm, tn), F32),
            pltpu.VMEM((tm, tn), F32),
            pltpu.VMEM((tm, tn), F32),
            pltpu.VMEM((tm, LANES), F32),
            pltpu.VMEM((tm, LANES), F32),
        ],
        compiler_params=_cparams(("parallel", "arbitrary")),
        name="in_proj",
    )(x2d, g1, w_in_blocks, qg, kg)


CONV_ROWS = 64


def _conv_kernel(cur_ref, prev_ref, st_ref, w_ref, b_ref, lg_ref, lb_ref,
                 c_ref, so_ref, buf, shifted, *, tt):
    ti = pl.program_id(1)
    dc = cur_ref.shape[2]

    @pl.when(ti == 0)
    def _():
        buf[0:CONV_PAD, :] = st_ref[0]

    @pl.when(ti > 0)
    def _():
        buf[0:CONV_PAD, :] = prev_ref[0]

    buf[CONV_PAD:CONV_PAD + tt, :] = cur_ref[0]
    lead = CONV_PAD - CONV_STATE
    rows = min(CONV_ROWS, tt)
    span = shifted.shape[1]
    for p in range(1, SUBLANES):
        shifted[p - 1] = buf[p:p + span, :]

    for rs in range(tt // rows):
        parts = []
        for cs in range(dc // LANES):
            sl = slice(cs * LANES, (cs + 1) * LANES)
            acc = jnp.zeros((rows, LANES), F32)
            for j in range(CONV_WIDTH):
                phase = (lead + j) % SUBLANES
                r0 = lead + j - phase + rs * rows
                src = buf if phase == 0 else shifted.at[phase - 1]
                acc = acc + src[r0:r0 + rows, sl] * w_ref[j:j + 1, sl]
            parts.append(acc + b_ref[:, sl])
        c = jnp.concatenate(parts, axis=1)
        mu = jnp.mean(c, axis=-1, keepdims=True)
        var = jnp.mean(jnp.square(c - mu), axis=-1, keepdims=True)
        y = (c - mu) * lax.rsqrt(var + EPS) * lg_ref[...] + lb_ref[...]
        c_ref[0, rs * rows:(rs + 1) * rows, :] = (y * jax.nn.sigmoid(y)).astype(BF16)

    @pl.when(ti == pl.num_programs(1) - 1)
    def _():
        so_ref[0] = buf[lead + tt:lead + tt + CONV_STATE, :]


def _conv(glu3d, hist, w, b, lg, lb, *, tt):
    bsz, t, dc = glu3d.shape
    assert t % tt == 0 and tt % min(CONV_ROWS, tt) == 0 and tt % CONV_PAD == 0
    per = tt // CONV_PAD
    return pl.pallas_call(
        functools.partial(_conv_kernel, tt=tt),
        grid=(bsz, t // tt),
        in_specs=[
            pl.BlockSpec((1, tt, dc), lambda bi, ti: (bi, ti, 0)),
            pl.BlockSpec((1, CONV_PAD, dc), lambda bi, ti: (bi, jnp.maximum(ti * per - 1, 0), 0)),
            pl.BlockSpec((1, CONV_PAD, dc), lambda bi, ti: (bi, 0, 0)),
            pl.BlockSpec((CONV_WIDTH, dc), lambda bi, ti: (0, 0)),
            pl.BlockSpec((1, dc), lambda bi, ti: (0, 0)),
            pl.BlockSpec((1, dc), lambda bi, ti: (0, 0)),
            pl.BlockSpec((1, dc), lambda bi, ti: (0, 0)),
        ],
        out_specs=(
            pl.BlockSpec((1, tt, dc), lambda bi, ti: (bi, ti, 0)),
            pl.BlockSpec((1, CONV_STATE, dc), lambda bi, ti: (bi, 0, 0)),
        ),
        out_shape=(
            jax.ShapeDtypeStruct((bsz, t, dc), BF16),
            jax.ShapeDtypeStruct((bsz, CONV_STATE, dc), F32),
        ),
        scratch_shapes=[
            pltpu.VMEM((CONV_PAD + tt, dc), F32),
            pltpu.VMEM((SUBLANES - 1, tt + CONV_PAD - SUBLANES, dc), F32),
        ],
        compiler_params=_cparams(("parallel", "arbitrary")),
        name="conv_branch",
    )(glu3d, glu3d, hist, w, b, lg, lb)


def _lambda_value(lq1_ref, lk1_ref, lq2_ref, lk2_ref, lam_init):
    a = jnp.sum(lq1_ref[...] * lk1_ref[...], axis=-1, keepdims=True)
    b = jnp.sum(lq2_ref[...] * lk2_ref[...], axis=-1, keepdims=True)
    return jnp.exp(a) - jnp.exp(b) + lam_init


def _head_finish(o1, o2, lam, sg, lam_init):
    o = o1 - lam * o2
    ms = jnp.mean(o * o, axis=-1, keepdims=True)
    return (o * lax.rsqrt(ms + EPS) * sg) * (1.0 - lam_init)


def _split_heads(q):
    lane = lax.broadcasted_iota(I32, q.shape, 1)
    zero = jnp.zeros_like(q)
    return jnp.where(lane < D_HEAD, q, zero), jnp.where(lane >= D_HEAD, q, zero)


ATTN_FULL_FULL, ATTN_FULL_DIAG, ATTN_DIAG = 0, 1, 2


def _attn_blocks(qcs, blocks, state):
    m, acc = state
    tq = qcs[0].shape[0]
    q2 = jnp.concatenate(qcs, axis=0)
    scores = []
    for kt, _, diagonal in blocks:
        s = jnp.dot(q2, kt, preferred_element_type=F32)
        if diagonal:
            tk = s.shape[1]
            row_chunk = (lax.broadcasted_iota(I32, (2 * tq, tk), 0) % tq) // CHUNK
            col_chunk = lax.broadcasted_iota(I32, (2 * tq, tk), 1) // CHUNK
            s = jnp.where(col_chunk <= row_chunk, s, NEG_INF)
        scores.append(s)
    for s, (_, v, _) in zip(scores, blocks):
        tk = s.shape[1]
        v_ext = jnp.concatenate([v, jnp.ones_like(v)], axis=1)
        m_new = jnp.maximum(m, jnp.max(s, axis=-1, keepdims=True))
        p = jnp.exp2(s - jnp.concatenate([m_new] * (tk // LANES), axis=1)).astype(BF16)
        alpha = jnp.exp2(m - m_new)
        acc = jnp.concatenate([alpha, alpha], axis=1) * acc + jnp.dot(p, v_ext, preferred_element_type=F32)
        m = m_new
    return m, acc


def _attn_prompt_kernel(qi_tab, ka_tab, kb_tab, kind_tab, q_ref, kta_ref, ktb_ref, va_ref, vb_ref,
                        lq1_ref, lk1_ref, lq2_ref, lk2_ref, sg_ref, uf_ref, vf_ref,
                        o_ref, ub_ref, vtb_ref, m_scr, acc_scr, *, lam_init):
    s_id = pl.program_id(2)
    kind = kind_tab[s_id]

    def convert():
        ub_ref[...] = uf_ref[...].astype(BF16)
        vtb_ref[0] = vf_ref[...].T.astype(BF16)

    @pl.when(ka_tab[s_id] == 0)
    def _():
        m_scr[...] = jnp.full(m_scr.shape, -jnp.inf, F32)
        acc_scr[...] = jnp.zeros(acc_scr.shape, F32)

    def load():
        return _split_heads(q_ref[0]), (m_scr[...], acc_scr[...])

    def store(state):
        m_scr[...] = state[0]
        acc_scr[...] = state[1]

    def finish(state):
        lam = _lambda_value(lq1_ref, lk1_ref, lq2_ref, lk2_ref, lam_init)
        tq = q_ref.shape[1]
        a1, a2 = state[1][:tq], state[1][tq:]
        y = _head_finish(a1[:, :D_VHEAD] / a1[:, D_VHEAD:], a2[:, :D_VHEAD] / a2[:, D_VHEAD:],
                         lam, sg_ref[...], lam_init)
        o_ref[0] = y.astype(BF16)

    @pl.when(kind == ATTN_FULL_FULL)
    def _():
        convert()
        qcs, state = load()
        store(_attn_blocks(qcs, [(kta_ref[0], va_ref[0], False), (ktb_ref[0], vb_ref[0], False)], state))

    @pl.when(kind == ATTN_FULL_DIAG)
    def _():
        convert()
        qcs, state = load()
        finish(_attn_blocks(qcs, [(kta_ref[0], va_ref[0], False), (ktb_ref[0], vb_ref[0], True)], state))

    @pl.when(kind == ATTN_DIAG)
    def _():
        convert()
        qcs, state = load()
        finish(_attn_blocks(qcs, [(kta_ref[0], va_ref[0], True)], state))


def _attn_steps(nq):
    steps = []
    for qi in range(nq):
        ki = 0
        while ki + 1 < qi:
            steps.append((qi, ki, ki + 1, ATTN_FULL_FULL))
            ki += 2
        if ki + 1 == qi:
            steps.append((qi, ki, qi, ATTN_FULL_DIAG))
        else:
            steps.append((qi, qi, qi, ATTN_DIAG))
    return steps


def _pow2_floor(n):
    return 1 << (n.bit_length() - 1)


def _attn_prompt(q3, kt3, v3, lq1, lk1, lq2, lk2, sg, peer_u, peer_v, *, tq, ce, lam_init):
    bsz, t, da = q3.shape
    ne, d = peer_u.shape
    nq = t // tq
    assert t % tq == 0 and tq % CHUNK == 0 and tq % LANES == 0
    steps = _attn_steps(nq)
    ns = len(steps)
    tabs = [jnp.asarray([st[i] for st in steps], I32) for i in range(4)]
    total = bsz * N_HEADS * ns
    n_u = min(_pow2_floor(total), ne // 16)
    n_v = min(_pow2_floor(total), ne // LANES)
    ru, rv = ne // n_u, ne // n_v
    assert ne % n_u == 0 and ne % n_v == 0 and ce % rv == 0
    flat = lambda b, h, s: (b * N_HEADS + h) * ns + s
    u_idx = lambda b, h, s, *_: (jnp.minimum(flat(b, h, s), n_u - 1), 0)
    v_idx = lambda b, h, s, *_: (jnp.minimum(flat(b, h, s), n_v - 1), 0)

    def vt_idx(b, h, s, *_):
        piece = jnp.minimum(flat(b, h, s), n_v - 1)
        return (piece // (ce // rv), 0, piece % (ce // rv))

    vec = lambda n: pl.BlockSpec((1, n), lambda b, h, s, *_: (0, 0))
    grid_spec = pltpu.PrefetchScalarGridSpec(
        num_scalar_prefetch=4,
        grid=(bsz, N_HEADS, ns),
        in_specs=[
            pl.BlockSpec((1, tq, LANES), lambda b, h, s, qt, ka, kb, kd: (b, qt[s], h)),
            pl.BlockSpec((1, LANES, tq), lambda b, h, s, qt, ka, kb, kd: (b, h, ka[s])),
            pl.BlockSpec((1, LANES, tq), lambda b, h, s, qt, ka, kb, kd: (b, h, kb[s])),
            pl.BlockSpec((1, tq, LANES), lambda b, h, s, qt, ka, kb, kd: (b, ka[s], h)),
            pl.BlockSpec((1, tq, LANES), lambda b, h, s, qt, ka, kb, kd: (b, kb[s], h)),
            vec(D_HEAD), vec(D_HEAD), vec(D_HEAD), vec(D_HEAD), vec(D_VHEAD),
            pl.BlockSpec((ru, d), u_idx),
            pl.BlockSpec((rv, d), v_idx),
        ],
        out_specs=(
            pl.BlockSpec((1, tq, LANES), lambda b, h, s, qt, ka, kb, kd: (b, qt[s], h)),
            pl.BlockSpec((ru, d), u_idx),
            pl.BlockSpec((1, d, rv), vt_idx),
        ),
        scratch_shapes=[
            pltpu.VMEM((2 * tq, LANES), F32),
            pltpu.VMEM((2 * tq, 2 * D_VHEAD), F32),
        ],
    )
    return pl.pallas_call(
        functools.partial(_attn_prompt_kernel, lam_init=lam_init),
        grid_spec=grid_spec,
        out_shape=(
            jax.ShapeDtypeStruct((bsz, t, da), BF16),
            jax.ShapeDtypeStruct((ne, d), BF16),
            jax.ShapeDtypeStruct((ne // ce, d, ce), BF16),
        ),
        compiler_params=_cparams(("arbitrary", "arbitrary", "arbitrary")),
        name="attn_prompt",
    )(*tabs, q3, kt3, kt3, v3, v3, lq1, lk1, lq2, lk2, sg, peer_u, peer_v)


def _attn_sample_kernel(q_ref, kn_ref, vn_ref, ck_ref, cv_ref, lq1_ref, lk1_ref, lq2_ref, lk2_ref, sg_ref,
                        o_ref, *, lam_init):
    past = ck_ref.shape[2]
    lam = _lambda_value(lq1_ref, lk1_ref, lq2_ref, lk2_ref, lam_init)
    for h in range(N_HEADS):
        sl = slice(h * LANES, (h + 1) * LANES)
        q = q_ref[0, :, sl]
        kn = kn_ref[0, :, sl]
        vn = vn_ref[0, :, sl]
        kpt = ck_ref[0, sl, :].astype(BF16)
        vp = cv_ref[0, pl.ds(h, past, stride=N_HEADS), :].astype(BF16)
        outs = []
        for qc in _split_heads(q):
            sp = jnp.dot(qc, kpt, preferred_element_type=F32)
            sn = lax.dot_general(qc, kn, NT_DIMS, preferred_element_type=F32)
            m = jnp.maximum(jnp.max(sp, axis=-1, keepdims=True), jnp.max(sn, axis=-1, keepdims=True))
            pp = jnp.exp2(sp - m)
            pn = jnp.exp2(sn - m)
            l = jnp.sum(pp, axis=-1, keepdims=True) + jnp.sum(pn, axis=-1, keepdims=True)
            o = (jnp.dot(pp.astype(BF16), vp, preferred_element_type=F32)
                 + jnp.dot(pn.astype(BF16), vn, preferred_element_type=F32))
            outs.append(o / l)
        o_ref[0, :, sl] = _head_finish(outs[0], outs[1], lam, sg_ref[...], lam_init).astype(BF16)


def _attn_sample(q3, kn3, vn3, ckt3, cv3, lq1, lk1, lq2, lk2, sg, *, lam_init):
    bsz, t, da = q3.shape
    past = ckt3.shape[2]
    vec = lambda n: pl.BlockSpec((1, n), lambda b: (0, 0))
    new = lambda: pl.BlockSpec((1, t, da), lambda b: (b, 0, 0))
    return pl.pallas_call(
        functools.partial(_attn_sample_kernel, lam_init=lam_init),
        grid=(bsz,),
        in_specs=[new(), new(), new(),
                  pl.BlockSpec((1, da, past), lambda b: (b, 0, 0)),
                  pl.BlockSpec((1, past * N_HEADS, D_VHEAD), lambda b: (b, 0, 0)),
                  vec(D_HEAD), vec(D_HEAD), vec(D_HEAD), vec(D_HEAD), vec(D_VHEAD)],
        out_specs=new(),
        out_shape=jax.ShapeDtypeStruct((bsz, t, da), BF16),
        compiler_params=_cparams(("parallel",)),
        name="attn_sample",
    )(q3, kn3, vn3, ckt3, cv3, lq1, lk1, lq2, lk2, sg)


def _merge_kernel(c_ref, o_ref, gc_ref, ga_ref, wco_ref, wao_ref, wout_ref, x_ref, y_ref):
    j = pl.program_id(1)

    @pl.when(j == 0)
    def _():
        y_ref[...] = x_ref[...]

    co = jnp.dot(c_ref[...], wco_ref[0], preferred_element_type=F32)
    ao = jnp.dot(o_ref[...], wao_ref[0], preferred_element_type=F32)
    merged = gc_ref[...].astype(F32) * co + ga_ref[...].astype(F32) * ao
    y_ref[...] += jnp.dot(merged.astype(BF16), wout_ref[...], preferred_element_type=F32)


def _merge(c2d, o2d, gates, wco, wao, wout, x2d, *, tm):
    n, d = x2d.shape
    dh = d // 2
    assert n % tm == 0
    return pl.pallas_call(
        _merge_kernel,
        grid=(n // tm, 2),
        in_specs=[
            pl.BlockSpec((tm, dh), lambda i, j: (i, 0)),
            pl.BlockSpec((tm, dh), lambda i, j: (i, 0)),
            pl.BlockSpec((tm, dh), lambda i, j: (i, j)),
            pl.BlockSpec((tm, dh), lambda i, j: (i, 2 + j)),
            pl.BlockSpec((1, dh, dh), lambda i, j: (j, 0, 0)),
            pl.BlockSpec((1, dh, dh), lambda i, j: (j, 0, 0)),
            pl.BlockSpec((dh, d), lambda i, j: (j, 0)),
            pl.BlockSpec((tm, d), lambda i, j: (i, 0)),
        ],
        out_specs=pl.BlockSpec((tm, d), lambda i, j: (i, 0)),
        out_shape=jax.ShapeDtypeStruct((n, d), F32),
        compiler_params=_cparams(("parallel", "arbitrary")),
        name="merge_out_proj",
    )(c2d, o2d, gates, gates, wco, wao, wout, x2d)


def _topk_rows(scores, cur_ref, rank_ref, val_ref, k):
    rows = scores.shape[0]

    cur_ref[...] = scores
    rank_ref[...] = jnp.full(rank_ref.shape, float(k), F32)

    def fast(j, m):
        cur = cur_ref[...]
        hit = cur == m
        rank_ref[...] = jnp.where(hit, lax.convert_element_type(j, F32), rank_ref[...])
        nxt = jnp.where(hit, -jnp.inf, cur)
        cur_ref[...] = nxt
        val_ref[j] = m
        return jnp.max(nxt, axis=0, keepdims=True)

    lax.fori_loop(0, k, fast, jnp.max(scores, axis=0, keepdims=True))
    ranked = jnp.sum(jnp.where(rank_ref[...] < float(k), 1.0, 0.0), axis=0, keepdims=True)
    tied = jnp.max(ranked) > float(k)

    @pl.when(tied)
    def _():
        _topk_rows_tie_aware(scores, cur_ref, rank_ref, val_ref, k)


def _topk_rows_tie_aware(scores, cur_ref, rank_ref, val_ref, k):
    rows = scores.shape[0]
    iota = lax.broadcasted_iota(I32, scores.shape, 0).astype(F32)
    cur_ref[...] = scores
    rank_ref[...] = jnp.full(rank_ref.shape, float(k), F32)

    def exact(j, carry):
        cur = cur_ref[...]
        m = jnp.max(cur, axis=0, keepdims=True)
        idx = jnp.min(jnp.where(cur == m, iota, float(rows)), axis=0, keepdims=True)
        hit = iota == idx
        rank_ref[...] = jnp.where(hit, lax.convert_element_type(j, F32), rank_ref[...])
        cur_ref[...] = jnp.where(hit, -jnp.inf, cur)
        val_ref[j] = m
        return carry

    lax.fori_loop(0, k, exact, 0)


def _cand_layout(kk):
    counts = [kk // (j1 + 1) for j1 in range(kk)]
    starts = [sum(counts[:j1]) for j1 in range(kk)]
    return counts, starts, sum(counts)


def _peer_route(x_ref, g2_ref, wq_ref, k1_ref, k2_ref, yt_scr, ht_scr, q_scr, e2_scr, rank2_scr, c_scr, n_scr,
                cur_scr, rank_scr, rank1_scr, cand_scr, crank_scr, val1_scr, val2_scr, valc_scr, *, tm):
    nk = k1_ref.shape[0]
    kk = PEER_TOPK
    counts, starts, ncand = _cand_layout(kk)
    x = x_ref[...]
    yt_scr[...] = jnp.zeros(yt_scr.shape, F32)
    ms = jnp.mean(x * x, axis=-1, keepdims=True)
    hf = x * lax.rsqrt(ms + EPS) * g2_ref[...]
    ht_scr[...] = hf.T.astype(BF16)
    q = jnp.dot(hf.astype(BF16), wq_ref[...], preferred_element_type=F32)
    for hd in range(PEER_HEADS):
        q_scr[hd] = q[:, hd * LANES:(hd + 1) * LANES]

    def head_body(hd, carry):
        qh = q_scr[hd]
        s1 = lax.dot_general(k1_ref[...], qh, NT_DIMS, preferred_element_type=F32)
        s2 = lax.dot_general(k2_ref[...], qh, NT_DIMS, preferred_element_type=F32)
        _topk_rows(s1, cur_scr, rank1_scr, val1_scr, kk)
        _topk_rows(s2, cur_scr, rank_scr, val2_scr, kk)
        rank2_scr[hd] = rank_scr[...].astype(BF16)
        v1max = val1_scr[0]
        v2max = val2_scr[0]
        v2all = jnp.concatenate([val2_scr[j2] for j2 in range(kk)], axis=0)
        pieces = [val1_scr[j1] + v2all[:counts[j1], :] for j1 in range(kk)]
        pieces.append(jnp.full((cand_scr.shape[0] - ncand, tm), -jnp.inf, F32))
        cand = jnp.concatenate(pieces, axis=0)
        _topk_rows(cand, cand_scr, crank_scr, valc_scr, kk)
        picked = crank_scr[...] < float(kk)
        z = jnp.sum(jnp.where(picked, jnp.exp(cand - valc_scr[0]), 0.0), axis=0, keepdims=True)
        crow = lax.broadcasted_iota(I32, cand.shape, 0)
        rank1 = rank1_scr[...].astype(BF16)
        n_by = jnp.zeros((nk, tm), BF16)
        for j1 in range(kk):
            in_row = (crow >= starts[j1]) & (crow < starts[j1] + counts[j1])
            n_j1 = jnp.sum(jnp.where(picked & in_row, 1.0, 0.0), axis=0, keepdims=True)
            n_by = n_by + jnp.where(rank1 == j1, n_j1.astype(BF16), 0)
        n_scr[hd] = n_by.astype(F32)
        c_scr[hd] = jnp.exp(s1 - v1max) / z
        e2_scr[hd] = jnp.exp(s2 - v2max).astype(BF16)
        return carry

    lax.fori_loop(0, PEER_HEADS, head_body, 0)


def _peer_scores(u_ref, ht_scr, at_scr):
    at_scr[...] = jnp.dot(u_ref[...], ht_scr[...], preferred_element_type=F32)


PEER_KEYS_PER_DOT = 2


def _peer_gates(chunk, at_scr, wt_scr, e2_scr, rank2_scr, c_scr, n_scr, *, tm, ce, nk):
    for blk in range(ce // nk):
        r = chunk * (ce // nk) + blk
        acc = jnp.zeros((nk, tm), BF16)
        for hd in range(PEER_HEADS):
            n_row = n_scr[hd, pl.ds(r, 1), :].astype(BF16)
            c_row = c_scr[hd, pl.ds(r, 1), :].astype(BF16)
            acc = acc + jnp.where(rank2_scr[hd] < n_row, e2_scr[hd] * c_row, 0)
        a = at_scr[blk * nk:(blk + 1) * nk, :]
        gelu = 0.5 * a * (1.0 + lax.erf(a * (2.0 ** -0.5)))
        wt_scr[blk * nk:(blk + 1) * nk, :] = acc * gelu.astype(BF16)


def _peer_combine(wt_scr, vt_ref, yt_scr, *, nk):
    per = PEER_KEYS_PER_DOT * nk
    total = None
    for g in range(wt_scr.shape[0] // per):
        rows = slice(g * per, (g + 1) * per)
        d = jnp.dot(vt_ref[0, :, rows], wt_scr[rows, :], preferred_element_type=F32)
        total = d if total is None else total + d
    yt_scr[...] += total


def _peer_kernel(x_ref, g2_ref, wq_ref, k1_ref, k2_ref, u_ref, vt_ref, y_ref,
                 ht_scr, yt_scr, q_scr, at0_scr, at1_scr, wt_scr, e2_scr, rank2_scr, c_scr, n_scr,
                 cur_scr, rank_scr, rank1_scr, cand_scr, crank_scr, val1_scr, val2_scr, valc_scr,
                 *, tm, ce):
    s = pl.program_id(1)
    last = pl.num_programs(1) - 1
    nk = k1_ref.shape[0]
    even = s % 2 == 0

    def step(score_into, mix_from):
        if mix_from is not None:
            _peer_gates(s - 1, mix_from, wt_scr, e2_scr, rank2_scr, c_scr, n_scr, tm=tm, ce=ce, nk=nk)
        if score_into is not None:
            _peer_scores(u_ref, ht_scr, score_into)
        if mix_from is not None:
            _peer_combine(wt_scr, vt_ref, yt_scr, nk=nk)

    @pl.when(s == 0)
    def _():
        _peer_route(x_ref, g2_ref, wq_ref, k1_ref, k2_ref, yt_scr, ht_scr, q_scr, e2_scr, rank2_scr, c_scr, n_scr,
                    cur_scr, rank_scr, rank1_scr, cand_scr, crank_scr, val1_scr, val2_scr, valc_scr, tm=tm)
        step(at0_scr, None)

    @pl.when((s > 0) & (s < last) & even)
    def _():
        step(at0_scr, at1_scr)

    @pl.when((s < last) & jnp.logical_not(even))
    def _():
        step(at1_scr, at0_scr)

    @pl.when(s == last)
    def _():
        step(None, at1_scr)
        y_ref[...] = x_ref[...] + yt_scr[...].T


def _peer(x2d, g2, wq, k1p, k2p, u_bf, vt_bf, *, tm, ce):
    n, d = x2d.shape
    ne = u_bf.shape[0]
    nk = k1p.shape[0]
    kk = PEER_TOPK
    nc = ne // ce
    ncand = -(-_cand_layout(kk)[2] // 8) * 8
    assert n % tm == 0 and ne % ce == 0 and ce % (2 * nk) == 0 and ne == nk * nk and nk == LANES
    assert nc % 2 == 0
    return pl.pallas_call(
        functools.partial(_peer_kernel, tm=tm, ce=ce),
        grid=(n // tm, nc + 1),
        in_specs=[
            pl.BlockSpec((tm, d), lambda i, s: (i, 0), pipeline_mode=pl.Buffered(1)),
            pl.BlockSpec((1, d), lambda i, s: (0, 0)),
            pl.BlockSpec(wq.shape, lambda i, s: (0, 0), pipeline_mode=pl.Buffered(1)),
            pl.BlockSpec((nk, LANES), lambda i, s: (0, 0)),
            pl.BlockSpec((nk, LANES), lambda i, s: (0, 0)),
            pl.BlockSpec((ce, d), lambda i, s: (jnp.minimum(s, nc - 1), 0)),
            pl.BlockSpec((1, d, ce), lambda i, s: (jnp.maximum(s - 1, 0), 0, 0)),
        ],
        out_specs=pl.BlockSpec((tm, d), lambda i, s: (i, 0)),
        out_shape=jax.ShapeDtypeStruct((n, d), F32),
        scratch_shapes=[
            pltpu.VMEM((d, tm), BF16),
            pltpu.VMEM((d, tm), F32),
            pltpu.VMEM((PEER_HEADS, tm, LANES), F32),
            pltpu.VMEM((ce, tm), F32),
            pltpu.VMEM((ce, tm), F32),
            pltpu.VMEM((ce, tm), BF16),
            pltpu.VMEM((PEER_HEADS, nk, tm), BF16),
            pltpu.VMEM((PEER_HEADS, nk, tm), BF16),
            pltpu.VMEM((PEER_HEADS, nk, tm), F32),
            pltpu.VMEM((PEER_HEADS, nk, tm), F32),
            pltpu.VMEM((nk, tm), F32),
            pltpu.VMEM((nk, tm), F32),
            pltpu.VMEM((nk, tm), F32),
            pltpu.VMEM((ncand, tm), F32),
            pltpu.VMEM((ncand, tm), F32),
            pltpu.VMEM((kk, 1, tm), F32),
            pltpu.VMEM((kk, 1, tm), F32),
            pltpu.VMEM((kk, 1, tm), F32),
        ],
        compiler_params=_cparams(("parallel", "arbitrary")),
        name="peer",
    )(x2d, g2, wq, k1p, k2p, u_bf, vt_bf)


def _tile128(g):
    return jnp.tile(g.reshape(1, -1), (1, LANES // g.shape[-1]))


def _layer(x3, pos_offset, hist, cache_k, cache_v, p, lam_init, *, tm, tt, tq, tm_peer, ce):
    bsz, t, d = x3.shape
    n = bsz * t
    x2d = x3.reshape(n, d)
    prompt = cache_k is None
    glu, q, k, kb, v, vb, gates = _in_proj(x2d, p["norm1_g"], p["w_in"], p["qg"], p["kg"],
                                           seq_len=t, pos_offset=pos_offset, tm=tm, k_transposed=prompt)
    dc = glu.shape[1]
    c, conv_state = _conv(glu.reshape(bsz, t, dc), hist, p["conv_dw_w"], p["conv_dw_b"],
                          p["conv_ln_g"], p["conv_ln_b"], tt=tt)
    lam_args = (p["lambda_q1"], p["lambda_k1"], p["lambda_q2"], p["lambda_k2"], p["subln_g"])
    r3 = lambda a: a.reshape(bsz, t, -1)
    if prompt:
        o, peer_u, peer_vt = _attn_prompt(r3(q), kb, r3(vb), *lam_args, p["peer_u_f32"], p["peer_v_f32"],
                                          tq=tq, ce=ce, lam_init=lam_init)
        p = dict(p, peer_u=peer_u, peer_vt=peer_vt)
        k = jnp.transpose(k.reshape(bsz, N_HEADS, 2, D_HEAD, t), (0, 4, 1, 2, 3))
    else:
        past = cache_k.shape[1]
        ckt = jnp.transpose(cache_k, (0, 2, 3, 4, 1)).reshape(bsz, -1, past)
        o = _attn_sample(r3(q), r3(kb), r3(vb), ckt, cache_v.reshape(bsz, past * N_HEADS, D_VHEAD), *lam_args,
                         lam_init=lam_init)
        k = k.reshape(bsz, t, N_HEADS, 2, D_HEAD)
    x_mid = _merge(c.reshape(n, dc), o.reshape(n, -1), gates, p["w_conv_out"], p["w_attn_out"],
                   p["w_out"], x2d, tm=tm)
    y = _peer(x_mid, p["norm2_g"], p["peer_wq"], p["k1p"], p["k2p"], p["peer_u"], p["peer_vt"],
              tm=tm_peer, ce=ce)
    return (y.reshape(bsz, t, d), k, v.reshape(bsz, t, N_HEADS, D_VHEAD), conv_state), p


def kernel(x_prompt, x_sample, cache_attn_k, cache_attn_v, state_conv, norm1_g, w_in, conv_dw_w, conv_dw_b,
           conv_ln_g, conv_ln_b, w_conv_out, q_norm_g, k_norm_g, lambda_q1, lambda_k1, lambda_q2, lambda_k2,
           subln_g, w_attn_out, w_out, norm2_g, peer_wq, peer_k1, peer_k2, peer_u, peer_v):
    depth = w_in.shape[0]
    xp, xs = x_prompt, x_sample
    outs = [[] for _ in range(6)]
    row = lambda a: a.reshape(1, -1)

    def col_blocks(w, width=None):
        return _bf16_col_blocks(w, width or w.shape[1], CAST_ROWS)

    bf16 = lambda w: col_blocks(w)[0]
    d_model = x_prompt.shape[-1]
    ce = PEER_CHUNK
    for l in range(depth):
        half = peer_k1.shape[-1]
        p = dict(
            norm1_g=row(norm1_g[l]), w_in=col_blocks(w_in[l], d_model // 2),
            conv_dw_w=conv_dw_w[l], conv_dw_b=row(conv_dw_b[l]),
            conv_ln_g=row(conv_ln_g[l]), conv_ln_b=row(conv_ln_b[l]),
            w_conv_out=col_blocks(w_conv_out[l], d_model // 2),
            qg=_tile128(q_norm_g[l]), kg=_tile128(k_norm_g[l]),
            lambda_q1=row(lambda_q1[l]), lambda_k1=row(lambda_k1[l]),
            lambda_q2=row(lambda_q2[l]), lambda_k2=row(lambda_k2[l]),
            subln_g=row(subln_g[l]),
            w_attn_out=col_blocks(w_attn_out[l], d_model // 2), w_out=bf16(w_out[l]),
            norm2_g=row(norm2_g[l]), peer_wq=bf16(peer_wq[l]),
            k1p=jnp.pad(peer_k1[l], ((0, 0), (0, LANES - half))),
            k2p=jnp.pad(peer_k2[l], ((0, 0), (LANES - half, 0))),
            peer_u_f32=peer_u[l], peer_v_f32=peer_v[l],
        )
        li = _lambda_init(l)
        bp, tp, _ = xp.shape
        bs, ts, _ = xs.shape
        zero_hist = jnp.zeros((bp, CONV_PAD, state_conv.shape[-1]), xp.dtype)
        (xp, kp, vp, cp), p = _layer(xp, 0, zero_hist, None, None, p, li,
                                     tm=min(512, bp * tp), tt=min(256, tp), tq=min(512, tp),
                                     tm_peer=min(512, bp * tp), ce=ce)
        hist_s = jnp.pad(state_conv[l], ((0, 0), (CONV_PAD - CONV_STATE, 0), (0, 0)))
        (xs, ks, vs, cs), _ = _layer(xs, cache_attn_k.shape[2], hist_s, cache_attn_k[l], cache_attn_v[l], p, li,
                                     tm=min(256, bs * ts), tt=ts, tq=ts,
                                     tm_peer=min(256, bs * ts), ce=ce)
        for lst, val in zip(outs, (kp, vp, cp, ks, vs, cs)):
            lst.append(val)
    kp, vp, cp, ks, vs, cs = (jnp.stack(o) for o in outs)
    return (xp, xs, kp, vp, cp, ks, vs, cs)
```
